```python
import math
import jax, jax.numpy as jnp
from jax import lax
import numpy as np

D_MODEL = 1024
BATCH = 8
SEQ = 2048
DEPTH = 1

DN_HEADS = 4
DN_HEAD_DIM = 128
DN_WIDTH = DN_HEADS * DN_HEAD_DIM
CONV_WIDTH = 4
CHUNK = 64
POOL_WINDOWS = (2, 4, 8, 16)
POOL_GROUPS = 4
POOL_GROUP_DIM = 128
POOL_WIDTH = POOL_GROUPS * POOL_GROUP_DIM
N_BRANCHES = 2
D_FF = 2816
ALPHA = (2.0 * DEPTH) ** 0.25
BETA_INIT = (8.0 * DEPTH) ** -0.25
LN_EPS = 1e-5
RMS_EPS = 1e-6
QKV_COLS = 3 * DN_WIDTH
Z_COLS = DN_WIDTH
B_COLS = DN_HEADS
A_COLS = DN_HEADS
POOL_COLS = POOL_WIDTH
GATE_COLS = N_BRANCHES * D_MODEL
IN_COLS = QKV_COLS + Z_COLS + B_COLS + A_COLS + POOL_COLS + GATE_COLS
IN_SPLITS = [QKV_COLS,
             QKV_COLS + Z_COLS,
             QKV_COLS + Z_COLS + B_COLS,
             QKV_COLS + Z_COLS + B_COLS + A_COLS,
             QKV_COLS + Z_COLS + B_COLS + A_COLS + POOL_COLS]

kernel_name = "hybrid_gated_deltanet_multiscale_pool_macaron_deepnorm"


def layer_norm(x, g, b):
    xf = x.astype(jnp.float32)
    mu = jnp.mean(xf, axis=-1, keepdims=True)
    var = jnp.mean(jnp.square(xf - mu), axis=-1, keepdims=True)
    return ((xf - mu) * lax.rsqrt(var + LN_EPS) * g + b).astype(x.dtype)


def swiglu(x, w_gate, w_up, w_down):
    return (jax.nn.silu(x @ w_gate) * (x @ w_up)) @ w_down


def causal_depthwise_conv_silu(x, w):
    K = w.shape[0]
    S_ = x.shape[1]
    xp = jnp.pad(x, ((0, 0), (K - 1, 0), (0, 0)))
    y = xp[:, 0:S_] * w[0]
    for j in range(1, K):
        y = y + xp[:, j:j + S_] * w[j]
    return jax.nn.silu(y)


def l2norm(x):
    return x * lax.rsqrt(jnp.sum(x * x, axis=-1, keepdims=True) + RMS_EPS)


def chunk_gated_delta_rule(q, k, v, g, beta):
    f32 = jnp.float32
    B_, S_, H, dk = q.shape
    dv = v.shape[-1]
    N = S_ // CHUNK
    q = l2norm(q.astype(f32)) * (dk ** -0.5)
    k = l2norm(k.astype(f32))
    v = v.astype(f32)
    g = g.astype(f32)
    beta = beta.astype(f32)

    def chunkify(t):
        t = t.reshape(B_, N, CHUNK, H, *t.shape[3:])
        return jnp.moveaxis(t, 3, 1)

    q, k, v, g, beta = map(chunkify, (q, k, v, g, beta))
    g = jnp.cumsum(g, axis=-1)
    causal = jnp.tril(jnp.ones((CHUNK, CHUNK), dtype=bool))
    strict = jnp.tril(jnp.ones((CHUNK, CHUNK), dtype=bool), -1)
    decay = jnp.exp(jnp.where(causal, g[..., :, None] - g[..., None, :], -jnp.inf))
    k_beta = k * beta[..., None]
    v_beta = v * beta[..., None]
    lower = jnp.where(strict, jnp.einsum('bhnik,bhnjk->bhnij', k_beta, k) * decay, 0.0)
    a_mat = lower + jnp.eye(CHUNK, dtype=f32)
    u = lax.linalg.triangular_solve(a_mat, v_beta, left_side=True, lower=True)
    w = lax.linalg.triangular_solve(a_mat, k_beta * jnp.exp(g)[..., None], left_side=True, lower=True)
    qk = jnp.where(causal, jnp.einsum('bhnik,bhnjk->bhnij', q, k) * decay, 0.0)
    q_dec = q * jnp.exp(g)[..., None]
    k_dec = k * jnp.exp(g[..., -1:] - g)[..., None]
    g_last = jnp.exp(g[..., -1])
    xs = tuple(jnp.moveaxis(t, 2, 0) for t in (qk, q_dec, k_dec, u, w, g_last))

    def step(state, inp):
        qk_c, qd_c, kd_c, u_c, w_c, gl_c = inp
        v_new = u_c - jnp.einsum('bhck,bhkv->bhcv', w_c, state)
        o_c = jnp.einsum('bhck,bhkv->bhcv', qd_c, state) + jnp.einsum('bhij,bhjv->bhiv', qk_c, v_new)
        state = state * gl_c[..., None, None] + jnp.einsum('bhck,bhcv->bhkv', kd_c, v_new)
        return state, o_c

    s0 = jnp.zeros((B_, H, dk, dv), f32)
    _, o = lax.scan(step, s0, xs)
    o = jnp.transpose(o, (1, 0, 3, 2, 4))
    return o.reshape(B_, S_, H, dv)


def multiscale_causal_pool(p):
    B_, S_, _ = p.shape
    pg = p.reshape(B_, S_, POOL_GROUPS, POOL_GROUP_DIM).astype(jnp.float32)
    csum = jnp.pad(jnp.cumsum(pg, axis=1), ((0, 0), (1, 0), (0, 0), (0, 0)))
    t = jnp.arange(S_)[:, None]
    win = jnp.array(POOL_WINDOWS, dtype=jnp.int32)[None, :]
    lo = jnp.maximum(t + 1 - win, 0)
    count = (t + 1 - lo).astype(jnp.float32)
    grp = jnp.arange(POOL_GROUPS)[None, :]
    mean = (csum[:, 1:] - csum[:, lo, grp]) / count[None, :, :, None]
    return (mean - pg).astype(p.dtype)


def hybrid_mixer(h, w_in, conv_w, a_log, dt_bias, dn_norm_g, dn_w_proj,
                 pool_w, pool_scale, pool_w_proj, w_out):
    B_, S_, _ = h.shape
    proj = h @ w_in
    qkv, z, b_raw, a_raw, p, gates = jnp.split(proj, IN_SPLITS, axis=-1)
    qkv = causal_depthwise_conv_silu(qkv, conv_w)
    q, k, v = jnp.split(qkv, 3, axis=-1)
    hd = (B_, S_, DN_HEADS, DN_HEAD_DIM)
    q, k, v = q.reshape(hd), k.reshape(hd), v.reshape(hd)
    beta = jax.nn.sigmoid(b_raw)
    g = -jnp.exp(a_log) * jax.nn.softplus(a_raw + dt_bias)
    o = chunk_gated_delta_rule(q, k, v, g, beta)
    o = o * lax.rsqrt(jnp.mean(o * o, axis=-1, keepdims=True) + RMS_EPS)
    o = o * dn_norm_g * jax.nn.silu(z.reshape(hd).astype(jnp.float32))
    y_dn = o.reshape(B_, S_, DN_WIDTH).astype(h.dtype) @ dn_w_proj
    pooled = multiscale_causal_pool(p)
    pooled = jnp.einsum('bsgc,gcd->bsgd', pooled, pool_w) * pool_scale
    y_pool = pooled.reshape(B_, S_, POOL_WIDTH) @ pool_w_proj
    g_dn, g_pool = jnp.split(gates, N_BRANCHES, axis=-1)
    merged = jax.nn.sigmoid(g_dn) * y_dn + jax.nn.sigmoid(g_pool) * y_pool
    return merged @ w_out


def setup_inputs(seed: int = 0) -> dict:
    key = jax.random.key(seed)
    ks = jax.random.split(key, 24)
    L = DEPTH
    nrm = lambda k, shape, scale: jax.random.normal(k, shape, jnp.float32) * scale
    gain = lambda k, shape: 1.0 + 0.02 * jax.random.normal(k, shape, jnp.float32)
    x = jax.random.normal(ks[0], (BATCH, SEQ, D_MODEL), jnp.float32)
    dt = jnp.exp(jax.random.uniform(ks[9], (L, DN_HEADS), jnp.float32,
                                    math.log(1e-3), math.log(1e-1)))
    return {
        "x": x,
        "ffn_pre_w_gate": nrm(ks[1], (L, D_MODEL, D_FF), D_MODEL ** -0.5),
        "ffn_pre_w_up": nrm(ks[2], (L, D_MODEL, D_FF), D_MODEL ** -0.5),
        "ffn_pre_w_down": nrm(ks[3], (L, D_FF, D_MODEL), D_FF ** -0.5 * BETA_INIT),
        "norm_pre_g": gain(ks[4], (L, D_MODEL)),
        "norm_pre_b": nrm(ks[5], (L, D_MODEL), 0.02),
        "mix_w_in": nrm(ks[6], (L, D_MODEL, IN_COLS), D_MODEL ** -0.5),
        "mix_conv_w": nrm(ks[7], (L, CONV_WIDTH, QKV_COLS), CONV_WIDTH ** -0.5),
        "dn_a_log": jnp.log(jax.random.uniform(ks[8], (L, DN_HEADS), jnp.float32, 1.0, 16.0)),
        "dn_dt_bias": dt + jnp.log(-jnp.expm1(-dt)),
        "dn_norm_g": gain(ks[10], (L, DN_HEAD_DIM)),
        "dn_w_proj": nrm(ks[11], (L, DN_WIDTH, D_MODEL), DN_WIDTH ** -0.5),
        "pool_w": nrm(ks[12], (L, POOL_GROUPS, POOL_GROUP_DIM, POOL_GROUP_DIM), POOL_GROUP_DIM ** -0.5),
        "pool_scale": 1.0 + 0.1 * jax.random.normal(ks[13], (L, POOL_GROUPS, POOL_GROUP_DIM), jnp.float32),
        "pool_w_proj": nrm(ks[14], (L, POOL_WIDTH, D_MODEL), POOL_WIDTH ** -0.5),
        "mix_w_out": nrm(ks[15], (L, D_MODEL, D_MODEL), D_MODEL ** -0.5 * BETA_INIT),
        "norm_mix_g": gain(ks[16], (L, D_MODEL)),
        "norm_mix_b": nrm(ks[17], (L, D_MODEL), 0.02),
        "ffn_post_w_gate": nrm(ks[18], (L, D_MODEL, D_FF), D_MODEL ** -0.5),
        "ffn_post_w_up": nrm(ks[19], (L, D_MODEL, D_FF), D_MODEL ** -0.5),
        "ffn_post_w_down": nrm(ks[20], (L, D_FF, D_MODEL), D_FF ** -0.5 * BETA_INIT),
        "norm_post_g": gain(ks[21], (L, D_MODEL)),
        "norm_post_b": nrm(ks[22], (L, D_MODEL), 0.02),
    }


def reference(x, ffn_pre_w_gate, ffn_pre_w_up, ffn_pre_w_down, norm_pre_g, norm_pre_b,
              mix_w_in, mix_conv_w, dn_a_log, dn_dt_bias, dn_norm_g, dn_w_proj,
              pool_w, pool_scale, pool_w_proj, mix_w_out, norm_mix_g, norm_mix_b,
              ffn_post_w_gate, ffn_post_w_up, ffn_post_w_down, norm_post_g, norm_post_b):
    h = x
    for l in range(DEPTH):
        f = swiglu(h, ffn_pre_w_gate[l], ffn_pre_w_up[l], ffn_pre_w_down[l])
        h = layer_norm(ALPHA * h + 0.5 * f, norm_pre_g[l], norm_pre_b[l])
        m = hybrid_mixer(h, mix_w_in[l], mix_conv_w[l], dn_a_log[l], dn_dt_bias[l], dn_norm_g[l],
                         dn_w_proj[l], pool_w[l], pool_scale[l], pool_w_proj[l], mix_w_out[l])
        h = layer_norm(ALPHA * h + m, norm_mix_g[l], norm_mix_b[l])
        f = swiglu(h, ffn_post_w_gate[l], ffn_post_w_up[l], ffn_post_w_down[l])
        h = layer_norm(ALPHA * h + 0.5 * f, norm_post_g[l], norm_post_b[l])
    return h
```

```python
import functools

import jax
import jax.numpy as jnp
from jax import lax
from jax.experimental import pallas as pl
from jax.experimental.pallas import tpu as pltpu

F32 = jnp.float32
BF16 = jnp.bfloat16

D_MODEL = 1024
D_FF = 2816
DN_HEADS = 4
DN_HEAD_DIM = 128
DN_WIDTH = DN_HEADS * DN_HEAD_DIM
CONV_WIDTH = 4
CHUNK = 64
POOL_WINDOWS = (2, 4, 8, 16)
POOL_GROUPS = 4
POOL_GROUP_DIM = 128
POOL_WIDTH = POOL_GROUPS * POOL_GROUP_DIM
QKV_COLS = 3 * DN_WIDTH
DEPTH = 1
ALPHA = (2.0 * DEPTH) ** 0.25
LN_EPS = 1e-5
RMS_EPS = 1e-6

SUBLANES = 8
LANES = 128
MIB = 1024 * 1024

FFN_ROWS = 512
FFN_COLS = 256
MIX_ROWS = 256
MIX_CHUNKS = MIX_ROWS // CHUNK
MAX_WINDOW = max(POOL_WINDOWS)
FFN_VMEM_LIMIT = 48 * MIB
MIX_VMEM_LIMIT = 56 * MIB


def _dot(a, b, **kw):
    return jnp.dot(a, b, preferred_element_type=F32, **kw)


def _dot_nt(a, b):
    return lax.dot_general(a, b, (((1,), (1,)), ((), ())), preferred_element_type=F32)


def _layer_norm(y, g, b):
    mu = jnp.mean(y, axis=-1, keepdims=True)
    yc = y - mu
    var = jnp.mean(yc * yc, axis=-1, keepdims=True)
    return yc * lax.rsqrt(var + LN_EPS) * g + b


def _silu(x):
    return x * jax.nn.sigmoid(x)


def _ffn_ln_kernel(x_ref, wgu_ref, wd_ref, g_ref, b_ref, o_ref):
    x = x_ref[...]
    xb = x.astype(BF16)
    acc = jnp.zeros(x.shape, F32)
    for c in range(D_FF // FFN_COLS):
        hu = _dot(xb, wgu_ref[:, 2 * c * FFN_COLS:2 * (c + 1) * FFN_COLS])
        a = (_silu(hu[:, :FFN_COLS]) * hu[:, FFN_COLS:]).astype(BF16)
        acc = acc + _dot(a, wd_ref[c * FFN_COLS:(c + 1) * FFN_COLS, :])
    o_ref[...] = _layer_norm(ALPHA * x + 0.5 * acc, g_ref[...], b_ref[...])


def _resident(shape):
    return pl.BlockSpec(shape, lambda *_: (0,) * len(shape), pipeline_mode=pl.Buffered(1))


def _ffn_ln(x2d, wgu, wd, g, b, name):
    t = x2d.shape[0]
    return pl.pallas_call(
        _ffn_ln_kernel,
        grid=(t // FFN_ROWS,),
        in_specs=[
            pl.BlockSpec((FFN_ROWS, D_MODEL), lambda i: (i, 0)),
            _resident((D_MODEL, 2 * D_FF)),
            _resident((D_FF, D_MODEL)),
            _resident((1, D_MODEL)),
            _resident((1, D_MODEL)),
        ],
        out_specs=pl.BlockSpec((FFN_ROWS, D_MODEL), lambda i: (i, 0)),
        out_shape=jax.ShapeDtypeStruct((t, D_MODEL), F32),
        compiler_params=pltpu.CompilerParams(
            dimension_semantics=("arbitrary",), vmem_limit_bytes=FFN_VMEM_LIMIT),
        name=name,
    )(x2d, wgu, wd, g, b)


def _mixer_kernel(h_ref, wqkv_ref, wz_ref, wba_ref, wp_ref, wgt_ref, convw_ref, prm_ref, dng_ref,
                  dnproj_ref, poolw_ref, pscale_ref, pproj_ref, wout_ref, lng_ref, lnb_ref,
                  o_ref, state_ref, xbuf_ref, ptail_ref):
    ts = MIX_ROWS
    tile = pl.program_id(1)

    @pl.when(tile == 0)
    def _():
        state_ref[...] = jnp.zeros(state_ref.shape, F32)
        xbuf_ref[0:SUBLANES, :] = jnp.zeros((SUBLANES, QKV_COLS), F32)
        ptail_ref[...] = jnp.zeros(ptail_ref.shape, F32)

    h = h_ref[0]
    hb = h.astype(BF16)

    qkv_raw = _dot(hb, wqkv_ref[...])
    xbuf_ref[SUBLANES:SUBLANES + ts, :] = qkv_raw
    cw = convw_ref[...]
    y = xbuf_ref[SUBLANES - 3:SUBLANES - 3 + ts, :] * cw[0:1]
    y = y + xbuf_ref[SUBLANES - 2:SUBLANES - 2 + ts, :] * cw[1:2]
    y = y + xbuf_ref[SUBLANES - 1:SUBLANES - 1 + ts, :] * cw[2:3]
    y = y + qkv_raw * cw[3:4]
    xbuf_ref[0:SUBLANES, :] = xbuf_ref[ts:ts + SUBLANES, :]
    qkv = _silu(y)

    ba = _dot(hb, wba_ref[...])
    beta = jax.nn.sigmoid(ba[:, :LANES])
    a_shift = ba[:, LANES:] + prm_ref[1:2, :]
    softplus = jnp.maximum(a_shift, 0.0) + jnp.log1p(jnp.exp(-jnp.abs(a_shift)))
    g = -jnp.exp(prm_ref[0:1, :]) * softplus

    ri = lax.broadcasted_iota(jnp.int32, (ts, ts), 0)
    ci = lax.broadcasted_iota(jnp.int32, (ts, ts), 1)
    same = (ri // CHUNK) == (ci // CHUNK)
    causal = same & (ri >= ci)
    strict = same & (ri > ci)
    gc = _dot(causal.astype(F32), g, precision=lax.Precision.HIGHEST)
    gr = gc.T
    eye = (ri == ci).astype(F32)

    z = _dot(hb, wz_ref[...])
    dng = dng_ref[...]
    o_heads = []
    for hd in range(DN_HEADS):
        lo = hd * DN_HEAD_DIM
        q = qkv[:, lo:lo + DN_HEAD_DIM]
        k = qkv[:, DN_WIDTH + lo:DN_WIDTH + lo + DN_HEAD_DIM]
        v = qkv[:, 2 * DN_WIDTH + lo:2 * DN_WIDTH + lo + DN_HEAD_DIM]
        qn = q * lax.rsqrt(jnp.sum(q * q, axis=-1, keepdims=True) + RMS_EPS) * (DN_HEAD_DIM ** -0.5)
        kn = k * lax.rsqrt(jnp.sum(k * k, axis=-1, keepdims=True) + RMS_EPS)
        beta_b = jnp.broadcast_to(beta[:, hd:hd + 1], (ts, DN_HEAD_DIM))
        gc_b = jnp.broadcast_to(gc[:, hd:hd + 1], (ts, DN_HEAD_DIM))
        decay = jnp.exp(jnp.where(causal, gc[:, hd:hd + 1] - gr[hd:hd + 1, :], -jnp.inf))
        eg_b = jnp.exp(gc_b)
        gl_b = jnp.concatenate(
            [jnp.broadcast_to(gc_b[c * CHUNK + CHUNK - 1:(c + 1) * CHUNK, :], (CHUNK, DN_HEAD_DIM))
             for c in range(MIX_CHUNKS)], axis=0)
        k_beta = kn * beta_b
        v_beta = v * beta_b
        kn_b = kn.astype(BF16)
        lower = jnp.where(strict, _dot_nt(k_beta.astype(BF16), kn_b) * decay, 0.0)
        t_mat = eye - lower
        power = lower.astype(BF16)
        for _ in range(5):
            power = _dot(power, power).astype(BF16)
            t_mat = t_mat + _dot(t_mat.astype(BF16), power)
        rhs = jnp.concatenate([v_beta, k_beta * eg_b], axis=1).astype(BF16)
        uw = _dot(t_mat.astype(BF16), rhs)
        u = uw[:, :DN_HEAD_DIM]
        w = uw[:, DN_HEAD_DIM:]
        qk = (_dot_nt(qn.astype(BF16), kn_b) * decay).astype(BF16)
        q_dec = qn * eg_b
        kd_t = (kn * jnp.exp(gl_b - gc_b)).T.astype(BF16)

        state = state_ref[hd]
        o_inter = []
        v_new_all = []
        for c in range(MIX_CHUNKS):
            r0 = c * CHUNK
            lhs = jnp.concatenate([w[r0:r0 + CHUNK], q_dec[r0:r0 + CHUNK]], axis=0).astype(BF16)
            ws = _dot(lhs, state.astype(BF16))
            v_new = u[r0:r0 + CHUNK] - ws[:CHUNK]
            o_inter.append(ws[CHUNK:])
            v_new_all.append(v_new)
            v_pad = jnp.concatenate(
                [v_new if cc == c else jnp.zeros((CHUNK, DN_HEAD_DIM), F32) for cc in range(MIX_CHUNKS)],
                axis=0).astype(BF16)
            g_last = eg_b[r0 + CHUNK - 1:r0 + CHUNK, :]
            state = state * g_last + _dot(kd_t, v_pad)
        state_ref[hd] = state
        v_new_full = jnp.concatenate(v_new_all, axis=0).astype(BF16)
        o = jnp.concatenate(o_inter, axis=0) + _dot(qk, v_new_full)
        o = o * lax.rsqrt(jnp.mean(o * o, axis=-1, keepdims=True) + RMS_EPS)
        o_heads.append(o * dng * _silu(z[:, lo:lo + DN_HEAD_DIM]))
    y_dn = _dot(jnp.concatenate(o_heads, axis=1).astype(BF16), dnproj_ref[...])

    p = _dot(hb, wp_ref[...])
    tail = ptail_ref[...]
    ptail_ref[...] = p[ts - MAX_WINDOW:, :]
    t_abs = tile * ts + lax.broadcasted_iota(jnp.int32, (ts, POOL_GROUP_DIM), 0)
    rt = lax.broadcasted_iota(jnp.int32, (MAX_WINDOW, MAX_WINDOW), 0)
    ct = lax.broadcasted_iota(jnp.int32, (MAX_WINDOW, MAX_WINDOW), 1)
    pooled = []
    for gi, win in enumerate(POOL_WINDOWS):
        lo = gi * POOL_GROUP_DIM
        pg = p[:, lo:lo + POOL_GROUP_DIM]
        in_win = ((ci <= ri) & (ci > ri - win)).astype(BF16)
        in_tail = (ct - MAX_WINDOW > rt - win).astype(BF16)
        wsum = _dot(in_win, pg.astype(BF16))
        top = wsum[:MAX_WINDOW] + _dot(in_tail, tail[:, lo:lo + POOL_GROUP_DIM].astype(BF16))
        wsum = jnp.concatenate([top, wsum[MAX_WINDOW:]], axis=0)
        count = jnp.minimum(t_abs + 1, win).astype(F32)
        pooled_g = wsum / count - pg
        pooled.append(_dot(pooled_g.astype(BF16), poolw_ref[gi]))
    pooled = jnp.concatenate(pooled, axis=1) * pscale_ref[...]
    y_pool = _dot(pooled.astype(BF16), pproj_ref[...])

    gates = _dot(hb, wgt_ref[...])
    merged = jax.nn.sigmoid(gates[:, :D_MODEL]) * y_dn + jax.nn.sigmoid(gates[:, D_MODEL:]) * y_pool
    m = _dot(merged.astype(BF16), wout_ref[...])
    o_ref[0] = _layer_norm(ALPHA * h + m, lng_ref[...], lnb_ref[...])


def _mixer_ln(h3d, wqkv, wz, wba, wp, wgt, convw, prm, dng, dnproj, poolw, pscale, pproj, wout, lng, lnb):
    b, s, _ = h3d.shape
    operands = (wqkv, wz, wba, wp, wgt, convw, prm, dng, dnproj, poolw, pscale, pproj, wout, lng, lnb)
    return pl.pallas_call(
        _mixer_kernel,
        grid=(b, s // MIX_ROWS),
        in_specs=[pl.BlockSpec((1, MIX_ROWS, D_MODEL), lambda i, j: (i, j, 0))]
        + [_resident(op.shape) for op in operands],
        out_specs=pl.BlockSpec((1, MIX_ROWS, D_MODEL), lambda i, j: (i, j, 0)),
        out_shape=jax.ShapeDtypeStruct(h3d.shape, F32),
        scratch_shapes=[
            pltpu.VMEM((DN_HEADS, DN_HEAD_DIM, DN_HEAD_DIM), F32),
            pltpu.VMEM((MIX_ROWS + SUBLANES, QKV_COLS), F32),
            pltpu.VMEM((MAX_WINDOW, POOL_WIDTH), F32),
        ],
        compiler_params=pltpu.CompilerParams(
            dimension_semantics=("arbitrary", "arbitrary"), vmem_limit_bytes=MIX_VMEM_LIMIT),
        name="mixer_ln",
    )(h3d, *operands)


def _gate_up_chunks(w_gate, w_up):
    n = D_FF // FFN_COLS
    wg = w_gate.reshape(D_MODEL, n, 1, FFN_COLS)
    wu = w_up.reshape(D_MODEL, n, 1, FFN_COLS)
    return jnp.concatenate([wg, wu], axis=2).reshape(D_MODEL, 2 * D_FF).astype(BF16)


def _row(v):
    return v.reshape(1, -1).astype(F32)


def kernel(x, ffn_pre_w_gate, ffn_pre_w_up, ffn_pre_w_down, norm_pre_g, norm_pre_b, mix_w_in, mix_conv_w, dn_a_log, dn_dt_bias, dn_norm_g, dn_w_proj, pool_w, pool_scale, pool_w_proj, mix_w_out, norm_mix_g, norm_mix_b, ffn_post_w_gate, ffn_post_w_up, ffn_post_w_down, norm_post_g, norm_post_b):
    bsz, seq, _ = x.shape
    h = x
    for l in range(DEPTH):
        h = _ffn_ln(h.reshape(bsz * seq, D_MODEL),
                    _gate_up_chunks(ffn_pre_w_gate[l], ffn_pre_w_up[l]),
                    ffn_pre_w_down[l].astype(BF16), _row(norm_pre_g[l]), _row(norm_pre_b[l]),
                    "ffn_ln_pre")

        w_in = mix_w_in[l]
        c0 = QKV_COLS
        c1 = c0 + DN_WIDTH
        c2 = c1 + DN_HEADS
        c3 = c2 + DN_HEADS
        c4 = c3 + POOL_WIDTH
        wba = jnp.zeros((D_MODEL, 2 * LANES), F32)
        wba = wba.at[:, :DN_HEADS].set(w_in[:, c1:c2]).at[:, LANES:LANES + DN_HEADS].set(w_in[:, c2:c3])
        prm = jnp.zeros((SUBLANES, LANES), F32)
        prm = prm.at[0, :DN_HEADS].set(dn_a_log[l]).at[1, :DN_HEADS].set(dn_dt_bias[l])
        h = _mixer_ln(
            h.reshape(bsz, seq, D_MODEL),
            w_in[:, :c0].astype(BF16), w_in[:, c0:c1].astype(BF16), wba.astype(BF16),
            w_in[:, c3:c4].astype(BF16), w_in[:, c4:].astype(BF16),
            mix_conv_w[l].astype(F32), prm, _row(dn_norm_g[l]),
            dn_w_proj[l].astype(BF16), pool_w[l].astype(BF16), _row(pool_scale[l]),
            pool_w_proj[l].astype(BF16), mix_w_out[l].astype(BF16),
            _row(norm_mix_g[l]), _row(norm_mix_b[l]))

        h = _ffn_ln(h.reshape(bsz * seq, D_MODEL),
                    _gate_up_chunks(ffn_post_w_gate[l], ffn_post_w_up[l]),
                    ffn_post_w_down[l].astype(BF16), _row(norm_post_g[l]), _row(norm_post_b[l]),
                    "ffn_ln_post")
    return h.reshape(bsz, seq, D_MODEL)
```

```python
import jax
import jax.numpy as jnp
from jax import lax
from jax.experimental import pallas as pl
from jax.experimental.pallas import tpu as pltpu

F32 = jnp.float32
BF16 = jnp.bfloat16

D_MODEL = 1024
D_FF = 2816
DN_HEADS = 4
DN_HEAD_DIM = 128
DN_WIDTH = DN_HEADS * DN_HEAD_DIM
CONV_WIDTH = 4
CHUNK = 64
POOL_WINDOWS = (2, 4, 8, 16)
POOL_GROUPS = 4
POOL_GROUP_DIM = 128
POOL_WIDTH = POOL_GROUPS * POOL_GROUP_DIM
QKV_COLS = 3 * DN_WIDTH
DEPTH = 1
ALPHA = (2.0 * DEPTH) ** 0.25
LN_EPS = 1e-5
RMS_EPS = 1e-6

SUBLANES = 8
LANES = 128
MIB = 1024 * 1024

FFN_ROWS = 512
FFN_COLS = 256
MIX_ROWS = 256
MIX_CHUNKS = MIX_ROWS // CHUNK
MAX_WINDOW = max(POOL_WINDOWS)
FFN_VMEM_LIMIT = 48 * MIB
MIX_VMEM_LIMIT = 56 * MIB


def _dot(a, b, **kw):
    return jnp.dot(a, b, preferred_element_type=F32, **kw)


def _dot_nt(a, b):
    return lax.dot_general(a, b, (((1,), (1,)), ((), ())), preferred_element_type=F32)


def _layer_norm(y, g, b):
    mu = jnp.mean(y, axis=-1, keepdims=True)
    yc = y - mu
    var = jnp.mean(yc * yc, axis=-1, keepdims=True)
    return yc * lax.rsqrt(var + LN_EPS) * g + b


def _silu(x):
    return x * jax.nn.sigmoid(x)


def _ffn_ln_kernel(x_ref, wg_ref, wu_ref, wd_ref, g_ref, b_ref, o_ref):
    x = x_ref[...]
    xb = x.astype(BF16)
    acc = jnp.zeros(x.shape, F32)
    for c in range(D_FF // FFN_COLS):
        cols = slice(c * FFN_COLS, (c + 1) * FFN_COLS)
        a = (_silu(_dot(xb, wg_ref[:, cols])) * _dot(xb, wu_ref[:, cols])).astype(BF16)
        acc = acc + _dot(a, wd_ref[cols, :])
    o_ref[...] = _layer_norm(ALPHA * x + 0.5 * acc, g_ref[...], b_ref[...])


def _resident(shape):
    return pl.BlockSpec(shape, lambda *_: (0,) * len(shape), pipeline_mode=pl.Buffered(1))


def _ffn_ln(x2d, wg, wu, wd, g, b, name):
    t = x2d.shape[0]
    return pl.pallas_call(
        _ffn_ln_kernel,
        grid=(t // FFN_ROWS,),
        in_specs=[
            pl.BlockSpec((FFN_ROWS, D_MODEL), lambda i: (i, 0)),
            _resident((D_MODEL, D_FF)),
            _resident((D_MODEL, D_FF)),
            _resident((D_FF, D_MODEL)),
            _resident((1, D_MODEL)),
            _resident((1, D_MODEL)),
        ],
        out_specs=pl.BlockSpec((FFN_ROWS, D_MODEL), lambda i: (i, 0)),
        out_shape=jax.ShapeDtypeStruct((t, D_MODEL), F32),
        compiler_params=pltpu.CompilerParams(
            dimension_semantics=("arbitrary",), vmem_limit_bytes=FFN_VMEM_LIMIT),
        name=name,
    )(x2d, wg, wu, wd, g, b)


def _mixer_kernel(h_ref, wqkv_ref, wz_ref, wba_ref, wp_ref, wgt_ref, convw_ref, prm_ref, dng_ref,
                  dnproj_ref, poolw_ref, pscale_ref, pproj_ref, wout_ref, lng_ref, lnb_ref,
                  o_ref, state_ref, xbuf_ref, ptail_ref):
    ts = MIX_ROWS
    tile = pl.program_id(1)

    @pl.when(tile == 0)
    def _():
        state_ref[...] = jnp.zeros(state_ref.shape, F32)
        xbuf_ref[0:SUBLANES, :] = jnp.zeros((SUBLANES, QKV_COLS), F32)
        ptail_ref[...] = jnp.zeros(ptail_ref.shape, F32)

    h = h_ref[0]
    hb = h.astype(BF16)

    qkv_raw = _dot(hb, wqkv_ref[...])
    xbuf_ref[SUBLANES:SUBLANES + ts, :] = qkv_raw
    cw = convw_ref[...]
    y = xbuf_ref[SUBLANES - 3:SUBLANES - 3 + ts, :] * cw[0:1]
    y = y + xbuf_ref[SUBLANES - 2:SUBLANES - 2 + ts, :] * cw[1:2]
    y = y + xbuf_ref[SUBLANES - 1:SUBLANES - 1 + ts, :] * cw[2:3]
    y = y + qkv_raw * cw[3:4]
    xbuf_ref[0:SUBLANES, :] = xbuf_ref[ts:ts + SUBLANES, :]
    qkv = _silu(y)

    ba = _dot(hb, wba_ref[...])
    beta = jax.nn.sigmoid(ba[:, :LANES])
    a_shift = ba[:, LANES:] + prm_ref[1:2, :]
    softplus = jnp.maximum(a_shift, 0.0) + jnp.log1p(jnp.exp(-jnp.abs(a_shift)))
    g = -jnp.exp(prm_ref[0:1, :]) * softplus

    ri = lax.broadcasted_iota(jnp.int32, (ts, ts), 0)
    ci = lax.broadcasted_iota(jnp.int32, (ts, ts), 1)
    same = (ri // CHUNK) == (ci // CHUNK)
    causal = same & (ri >= ci)
    strict = same & (ri > ci)
    gc = _dot(causal.astype(F32), g, precision=lax.Precision.HIGHEST)
    gr = gc.T
    eye = (ri == ci).astype(F32)

    z = _dot(hb, wz_ref[...])
    dng = dng_ref[...]
    heads = range(DN_HEADS)

    qn, kn, kn_b, k_beta, v_beta, decay, eg_b, kd_t = [], [], [], [], [], [], [], []
    for hd in heads:
        lo = hd * DN_HEAD_DIM
        q = qkv[:, lo:lo + DN_HEAD_DIM]
        k = qkv[:, DN_WIDTH + lo:DN_WIDTH + lo + DN_HEAD_DIM]
        v = qkv[:, 2 * DN_WIDTH + lo:2 * DN_WIDTH + lo + DN_HEAD_DIM]
        qn.append(q * lax.rsqrt(jnp.sum(q * q, axis=-1, keepdims=True) + RMS_EPS) * (DN_HEAD_DIM ** -0.5))
        kn.append(k * lax.rsqrt(jnp.sum(k * k, axis=-1, keepdims=True) + RMS_EPS))
        beta_b = jnp.broadcast_to(beta[:, hd:hd + 1], (ts, DN_HEAD_DIM))
        gc_b = jnp.broadcast_to(gc[:, hd:hd + 1], (ts, DN_HEAD_DIM))
        decay.append(jnp.exp(jnp.where(causal, gc[:, hd:hd + 1] - gr[hd:hd + 1, :], -jnp.inf)))
        eg_b.append(jnp.exp(gc_b))
        gl_b = jnp.concatenate(
            [jnp.broadcast_to(gc_b[c * CHUNK + CHUNK - 1:(c + 1) * CHUNK, :], (CHUNK, DN_HEAD_DIM))
             for c in range(MIX_CHUNKS)], axis=0)
        k_beta.append(kn[hd] * beta_b)
        v_beta.append(v * beta_b)
        kn_b.append(kn[hd].astype(BF16))
        kd_t.append((kn[hd] * jnp.exp(gl_b - gc_b)).T.astype(BF16))

    lower = [jnp.where(strict, _dot_nt(k_beta[hd].astype(BF16), kn_b[hd]) * decay[hd], 0.0) for hd in heads]
    qk = [(_dot_nt(qn[hd].astype(BF16), kn_b[hd]) * decay[hd]).astype(BF16) for hd in heads]
    t_mat = [eye - lower[hd] for hd in heads]
    power = [lower[hd].astype(BF16) for hd in heads]
    for _ in range(5):
        power = [_dot(power[hd], power[hd]).astype(BF16) for hd in heads]
        t_mat = [t_mat[hd] + _dot(t_mat[hd].astype(BF16), power[hd]) for hd in heads]
    uw = [_dot(t_mat[hd].astype(BF16),
               jnp.concatenate([v_beta[hd], k_beta[hd] * eg_b[hd]], axis=1).astype(BF16)) for hd in heads]
    q_dec = [qn[hd] * eg_b[hd] for hd in heads]

    state = [state_ref[hd] for hd in heads]
    o_inter = [[] for _ in heads]
    v_new_all = [[] for _ in heads]
    for c in range(MIX_CHUNKS):
        r0 = c * CHUNK
        for hd in heads:
            u_c = uw[hd][r0:r0 + CHUNK, :DN_HEAD_DIM]
            w_c = uw[hd][r0:r0 + CHUNK, DN_HEAD_DIM:]
            lhs = jnp.concatenate([w_c, q_dec[hd][r0:r0 + CHUNK]], axis=0).astype(BF16)
            ws = _dot(lhs, state[hd].astype(BF16))
            v_new = u_c - ws[:CHUNK]
            o_inter[hd].append(ws[CHUNK:])
            v_new_all[hd].append(v_new)
            v_pad = jnp.concatenate(
                [v_new if cc == c else jnp.zeros((CHUNK, DN_HEAD_DIM), F32) for cc in range(MIX_CHUNKS)],
                axis=0).astype(BF16)
            g_last = eg_b[hd][r0 + CHUNK - 1:r0 + CHUNK, :]
            state[hd] = state[hd] * g_last + _dot(kd_t[hd], v_pad)
    o_heads = []
    for hd in heads:
        lo = hd * DN_HEAD_DIM
        state_ref[hd] = state[hd]
        v_new_full = jnp.concatenate(v_new_all[hd], axis=0).astype(BF16)
        o = jnp.concatenate(o_inter[hd], axis=0) + _dot(qk[hd], v_new_full)
        o = o * lax.rsqrt(jnp.mean(o * o, axis=-1, keepdims=True) + RMS_EPS)
        o_heads.append(o * dng * _silu(z[:, lo:lo + DN_HEAD_DIM]))
    y_dn = _dot(jnp.concatenate(o_heads, axis=1).astype(BF16), dnproj_ref[...])

    p = _dot(hb, wp_ref[...])
    tail = ptail_ref[...]
    ptail_ref[...] = p[ts - MAX_WINDOW:, :]
    t_abs = tile * ts + lax.broadcasted_iota(jnp.int32, (ts, POOL_GROUP_DIM), 0)
    rt = lax.broadcasted_iota(jnp.int32, (MAX_WINDOW, MAX_WINDOW), 0)
    ct = lax.broadcasted_iota(jnp.int32, (MAX_WINDOW, MAX_WINDOW), 1)
    pooled = []
    for gi, win in enumerate(POOL_WINDOWS):
        lo = gi * POOL_GROUP_DIM
        pg = p[:, lo:lo + POOL_GROUP_DIM]
        in_win = ((ci <= ri) & (ci > ri - win)).astype(BF16)
        in_tail = (ct - MAX_WINDOW > rt - win).astype(BF16)
        wsum = _dot(in_win, pg.astype(BF16))
        top = wsum[:MAX_WINDOW] + _dot(in_tail, tail[:, lo:lo + POOL_GROUP_DIM].astype(BF16))
        wsum = jnp.concatenate([top, wsum[MAX_WINDOW:]], axis=0)
        count = jnp.minimum(t_abs + 1, win).astype(F32)
        pooled_g = wsum / count - pg
        pooled.append(_dot(pooled_g.astype(BF16), poolw_ref[gi]))
    pooled = jnp.concatenate(pooled, axis=1) * pscale_ref[...]
    y_pool = _dot(pooled.astype(BF16), pproj_ref[...])

    gates = _dot(hb, wgt_ref[...])
    merged = jax.nn.sigmoid(gates[:, :D_MODEL]) * y_dn + jax.nn.sigmoid(gates[:, D_MODEL:]) * y_pool
    m = _dot(merged.astype(BF16), wout_ref[...])
    o_ref[0] = _layer_norm(ALPHA * h + m, lng_ref[...], lnb_ref[...])


def _mixer_ln(h3d, wqkv, wz, wba, wp, wgt, convw, prm, dng, dnproj, poolw, pscale, pproj, wout, lng, lnb):
    b, s, _ = h3d.shape
    operands = (wqkv, wz, wba, wp, wgt, convw, prm, dng, dnproj, poolw, pscale, pproj, wout, lng, lnb)
    return pl.pallas_call(
        _mixer_kernel,
        grid=(b, s // MIX_ROWS),
        in_specs=[pl.BlockSpec((1, MIX_ROWS, D_MODEL), lambda i, j: (i, j, 0))]
        + [_resident(op.shape) for op in operands],
        out_specs=pl.BlockSpec((1, MIX_ROWS, D_MODEL), lambda i, j: (i, j, 0)),
        out_shape=jax.ShapeDtypeStruct(h3d.shape, F32),
        scratch_shapes=[
            pltpu.VMEM((DN_HEADS, DN_HEAD_DIM, DN_HEAD_DIM), F32),
            pltpu.VMEM((MIX_ROWS + SUBLANES, QKV_COLS), F32),
            pltpu.VMEM((MAX_WINDOW, POOL_WIDTH), F32),
        ],
        compiler_params=pltpu.CompilerParams(
            dimension_semantics=("arbitrary", "arbitrary"), vmem_limit_bytes=MIX_VMEM_LIMIT),
        name="mixer_ln",
    )(h3d, *operands)


def _row(v):
    return v.reshape(1, -1).astype(F32)


def kernel(x, ffn_pre_w_gate, ffn_pre_w_up, ffn_pre_w_down, norm_pre_g, norm_pre_b, mix_w_in, mix_conv_w, dn_a_log, dn_dt_bias, dn_norm_g, dn_w_proj, pool_w, pool_scale, pool_w_proj, mix_w_out, norm_mix_g, norm_mix_b, ffn_post_w_gate, ffn_post_w_up, ffn_post_w_down, norm_post_g, norm_post_b):
    bsz, seq, _ = x.shape
    h = x
    for l in range(DEPTH):
        h = _ffn_ln(h.reshape(bsz * seq, D_MODEL),
                    ffn_pre_w_gate[l].astype(BF16), ffn_pre_w_up[l].astype(BF16),
                    ffn_pre_w_down[l].astype(BF16), _row(norm_pre_g[l]), _row(norm_pre_b[l]),
                    "ffn_ln_pre")

        w_in = mix_w_in[l]
        c0 = QKV_COLS
        c1 = c0 + DN_WIDTH
        c2 = c1 + DN_HEADS
        c3 = c2 + DN_HEADS
        c4 = c3 + POOL_WIDTH
        wba = jnp.zeros((D_MODEL, 2 * LANES), F32)
        wba = wba.at[:, :DN_HEADS].set(w_in[:, c1:c2]).at[:, LANES:LANES + DN_HEADS].set(w_in[:, c2:c3])
        prm = jnp.zeros((SUBLANES, LANES), F32)
        prm = prm.at[0, :DN_HEADS].set(dn_a_log[l]).at[1, :DN_HEADS].set(dn_dt_bias[l])
        h = _mixer_ln(
            h.reshape(bsz, seq, D_MODEL),
            w_in[:, :c0].astype(BF16), w_in[:, c0:c1].astype(BF16), wba.astype(BF16),
            w_in[:, c3:c4].astype(BF16), w_in[:, c4:].astype(BF16),
            mix_conv_w[l].astype(F32), prm, _row(dn_norm_g[l]),
            dn_w_proj[l].astype(BF16), pool_w[l].astype(BF16), _row(pool_scale[l]),
            pool_w_proj[l].astype(BF16), mix_w_out[l].astype(BF16),
            _row(norm_mix_g[l]), _row(norm_mix_b[l]))

        h = _ffn_ln(h.reshape(bsz * seq, D_MODEL),
                    ffn_post_w_gate[l].astype(BF16), ffn_post_w_up[l].astype(BF16),
                    ffn_post_w_down[l].astype(BF16), _row(norm_post_g[l]), _row(norm_post_b[l]),
                    "ffn_ln_post")
    return h.reshape(bsz, seq, D_MODEL)
```

```python
import numpy as np

import jax
import jax.numpy as jnp
from jax import lax
from jax.experimental import pallas as pl
from jax.experimental.pallas import tpu as pltpu

F32 = jnp.float32
BF16 = jnp.bfloat16

D_MODEL = 1024
D_FF = 2816
DN_HEADS = 4
DN_HEAD_DIM = 128
DN_WIDTH = DN_HEADS * DN_HEAD_DIM
CONV_WIDTH = 4
CHUNK = 64
POOL_WINDOWS = (2, 4, 8, 16)
POOL_GROUPS = 4
POOL_GROUP_DIM = 128
POOL_WIDTH = POOL_GROUPS * POOL_GROUP_DIM
QKV_COLS = 3 * DN_WIDTH
DEPTH = 1
ALPHA = (2.0 * DEPTH) ** 0.25
LN_EPS = 1e-5
RMS_EPS = 1e-6

SUBLANES = 8
LANES = 128
MIB = 1024 * 1024

FFN_ROWS = 512
FFN_COLS = 256
MIX_ROWS = 256
MIX_CHUNKS = MIX_ROWS // CHUNK
MAX_WINDOW = max(POOL_WINDOWS)
FFN_VMEM_LIMIT = 48 * MIB
MIX_VMEM_LIMIT = 56 * MIB


def _dot(a, b, **kw):
    return jnp.dot(a, b, preferred_element_type=F32, **kw)


def _dot_nt(a, b):
    return lax.dot_general(a, b, (((1,), (1,)), ((), ())), preferred_element_type=F32)


def _layer_norm(y, g, b):
    mu = jnp.mean(y, axis=-1, keepdims=True)
    yc = y - mu
    var = jnp.mean(yc * yc, axis=-1, keepdims=True)
    return yc * lax.rsqrt(var + LN_EPS) * g + b


def _silu(x):
    return x * jax.nn.sigmoid(x)


def _ffn_ln_kernel(x_ref, wg_ref, wu_ref, wd_ref, g_ref, b_ref, o_ref):
    x = x_ref[...]
    xb = x.astype(BF16)
    acc = jnp.zeros(x.shape, F32)
    for c in range(D_FF // FFN_COLS):
        cols = slice(c * FFN_COLS, (c + 1) * FFN_COLS)
        a = (_silu(_dot(xb, wg_ref[:, cols])) * _dot(xb, wu_ref[:, cols])).astype(BF16)
        acc = acc + _dot(a, wd_ref[cols, :])
    o_ref[...] = _layer_norm(ALPHA * x + 0.5 * acc, g_ref[...], b_ref[...])


def _resident(shape):
    return pl.BlockSpec(shape, lambda *_: (0,) * len(shape), pipeline_mode=pl.Buffered(1))


def _ffn_ln(x2d, wg, wu, wd, g, b, name):
    t = x2d.shape[0]
    return pl.pallas_call(
        _ffn_ln_kernel,
        grid=(t // FFN_ROWS,),
        in_specs=[
            pl.BlockSpec((FFN_ROWS, D_MODEL), lambda i: (i, 0)),
            _resident((D_MODEL, D_FF)),
            _resident((D_MODEL, D_FF)),
            _resident((D_FF, D_MODEL)),
            _resident((1, D_MODEL)),
            _resident((1, D_MODEL)),
        ],
        out_specs=pl.BlockSpec((FFN_ROWS, D_MODEL), lambda i: (i, 0)),
        out_shape=jax.ShapeDtypeStruct((t, D_MODEL), F32),
        compiler_params=pltpu.CompilerParams(
            dimension_semantics=("arbitrary",), vmem_limit_bytes=FFN_VMEM_LIMIT),
        name=name,
    )(x2d, wg, wu, wd, g, b)


def _mixer_masks():
    r = np.arange(MIX_ROWS)[:, None]
    c = np.arange(MIX_ROWS)[None, :]
    block_diag = (r // CHUNK == c // CHUNK)
    in_win = np.stack([(c <= r) & (c > r - w) for w in POOL_WINDOWS])
    rt = np.arange(MAX_WINDOW)[:, None]
    ct = np.arange(MAX_WINDOW)[None, :]
    in_tail = np.stack([(ct - MAX_WINDOW > rt - w) for w in POOL_WINDOWS])
    as_bf16 = lambda m: jnp.asarray(m.astype(np.float32), dtype=BF16)
    return as_bf16(block_diag), as_bf16(in_win), as_bf16(in_tail)


def _mixer_kernel(h_ref, wqkv_ref, wz_ref, wba_ref, wp_ref, wgt_ref, convw_ref, prm_ref, dng_ref,
                  dnproj_ref, poolw_ref, pscale_ref, pproj_ref, wout_ref, lng_ref, lnb_ref,
                  bdmask_ref, inwin_ref, intail_ref,
                  o_ref, state_ref, xbuf_ref, ptail_ref):
    ts = MIX_ROWS
    tile = pl.program_id(1)

    @pl.when(tile == 0)
    def _():
        state_ref[...] = jnp.zeros(state_ref.shape, F32)
        xbuf_ref[0:SUBLANES, :] = jnp.zeros((SUBLANES, QKV_COLS), F32)
        ptail_ref[...] = jnp.zeros(ptail_ref.shape, F32)

    h = h_ref[0]
    hb = h.astype(BF16)

    ba = _dot(hb, wba_ref[...])
    beta = jax.nn.sigmoid(ba[:, :LANES])
    a_shift = ba[:, LANES:] + prm_ref[1:2, :]
    softplus = jnp.maximum(a_shift, 0.0) + jnp.log1p(jnp.exp(-jnp.abs(a_shift)))
    g = -jnp.exp(prm_ref[0:1, :]) * softplus
    gr = g.T[0:SUBLANES]
    lane_in_chunk = lax.broadcasted_iota(jnp.int32, (SUBLANES, ts), 1) % CHUNK
    shift = 1
    while shift < CHUNK:
        rolled = jnp.concatenate(
            [pltpu.roll(gr[:, i * LANES:(i + 1) * LANES], shift, 1) for i in range(ts // LANES)], axis=1)
        gr = gr + jnp.where(lane_in_chunk >= shift, rolled, 0.0)
        shift *= 2
    gc = jnp.concatenate([gr, jnp.zeros((LANES - SUBLANES, ts), F32)], axis=0).T

    qkv_raw = _dot(hb, wqkv_ref[...])
    xbuf_ref[SUBLANES:SUBLANES + ts, :] = qkv_raw
    cw = convw_ref[...]
    y = xbuf_ref[SUBLANES - 3:SUBLANES - 3 + ts, :] * cw[0:1]
    y = y + xbuf_ref[SUBLANES - 2:SUBLANES - 2 + ts, :] * cw[1:2]
    y = y + xbuf_ref[SUBLANES - 1:SUBLANES - 1 + ts, :] * cw[2:3]
    y = y + qkv_raw * cw[3:4]
    xbuf_ref[0:SUBLANES, :] = xbuf_ref[ts:ts + SUBLANES, :]
    qkv = _silu(y)

    p = _dot(hb, wp_ref[...])
    tail = ptail_ref[...]
    ptail_ref[...] = p[ts - MAX_WINDOW:, :]
    t_abs = tile * ts + lax.broadcasted_iota(jnp.int32, (ts, POOL_GROUP_DIM), 0)
    pooled = []
    for gi, win in enumerate(POOL_WINDOWS):
        lo = gi * POOL_GROUP_DIM
        pg = p[:, lo:lo + POOL_GROUP_DIM]
        wsum = _dot(inwin_ref[gi], pg.astype(BF16))
        top = wsum[:MAX_WINDOW] + _dot(intail_ref[gi], tail[:, lo:lo + POOL_GROUP_DIM].astype(BF16))
        wsum = jnp.concatenate([top, wsum[MAX_WINDOW:]], axis=0)
        count = jnp.minimum(t_abs + 1, win).astype(F32)
        pooled_g = wsum / count - pg
        pooled.append(_dot(pooled_g.astype(BF16), poolw_ref[gi]))
    pooled = jnp.concatenate(pooled, axis=1) * pscale_ref[...]
    y_pool = _dot(pooled.astype(BF16), pproj_ref[...])
    gates = _dot(hb, wgt_ref[...])
    gated_pool = jax.nn.sigmoid(gates[:, D_MODEL:]) * y_pool
    gate_dn = jax.nn.sigmoid(gates[:, :D_MODEL])
    z = _dot(hb, wz_ref[...])

    dng = dng_ref[...]
    bdmask = bdmask_ref[...]
    heads = range(DN_HEADS)
    prow = lax.broadcasted_iota(jnp.int32, (CHUNK, ts), 0)
    plane = lax.broadcasted_iota(jnp.int32, (CHUNK, ts), 1)
    pcol = plane % CHUNK
    in_block = [plane // CHUNK == c for c in range(MIX_CHUNKS)]
    causal_p = prow >= pcol
    strict_p = prow > pcol
    eye_p = (prow == pcol).astype(F32)

    def pack(full):
        out = full[0:CHUNK]
        for c in range(1, MIX_CHUNKS):
            out = jnp.where(in_block[c], full[c * CHUNK:(c + 1) * CHUNK], out)
        return out

    def block_diag(packed):
        return jnp.concatenate([packed.astype(BF16)] * MIX_CHUNKS, axis=0) * bdmask

    qn, kn_b, k_beta, v_beta, eg_b, kd_t, decay_p = [], [], [], [], [], [], []
    for hd in heads:
        lo = hd * DN_HEAD_DIM
        q = qkv[:, lo:lo + DN_HEAD_DIM]
        k = qkv[:, DN_WIDTH + lo:DN_WIDTH + lo + DN_HEAD_DIM]
        v = qkv[:, 2 * DN_WIDTH + lo:2 * DN_WIDTH + lo + DN_HEAD_DIM]
        qn.append(q * lax.rsqrt(jnp.sum(q * q, axis=-1, keepdims=True) + RMS_EPS) * (DN_HEAD_DIM ** -0.5))
        kn = k * lax.rsqrt(jnp.sum(k * k, axis=-1, keepdims=True) + RMS_EPS)
        beta_b = jnp.broadcast_to(beta[:, hd:hd + 1], (ts, DN_HEAD_DIM))
        gc_b = jnp.broadcast_to(gc[:, hd:hd + 1], (ts, DN_HEAD_DIM))
        gc_p = jnp.broadcast_to(gc[0:CHUNK, hd:hd + 1], (CHUNK, ts))
        for c in range(1, MIX_CHUNKS):
            gc_p = jnp.where(in_block[c],
                             jnp.broadcast_to(gc[c * CHUNK:(c + 1) * CHUNK, hd:hd + 1], (CHUNK, ts)), gc_p)
        decay_p.append(jnp.exp(jnp.where(causal_p, gc_p - gr[hd:hd + 1, :], -jnp.inf)))
        eg_b.append(jnp.exp(gc_b))
        gl_b = jnp.concatenate(
            [jnp.broadcast_to(gc_b[c * CHUNK + CHUNK - 1:(c + 1) * CHUNK, :], (CHUNK, DN_HEAD_DIM))
             for c in range(MIX_CHUNKS)], axis=0)
        k_beta.append(kn * beta_b)
        v_beta.append(v * beta_b)
        kn_b.append(kn.astype(BF16))
        kd_t.append((kn * jnp.exp(gl_b - gc_b)).T.astype(BF16))

    gram = [_dot_nt(jnp.concatenate([k_beta[hd], qn[hd]], axis=0).astype(BF16), kn_b[hd]) for hd in heads]
    lower = [pack(gram[hd][:ts]) * jnp.where(strict_p, decay_p[hd], 0.0) for hd in heads]
    qk = [block_diag(pack(gram[hd][ts:]) * decay_p[hd]) for hd in heads]
    t_mat = [eye_p - lower[hd] for hd in heads]
    power = [_dot(lower[hd].astype(BF16), block_diag(lower[hd])) for hd in heads]
    for _ in range(4):
        both = [_dot(jnp.concatenate([power[hd], t_mat[hd]], axis=0).astype(BF16), block_diag(power[hd]))
                for hd in heads]
        power = [both[hd][:CHUNK] for hd in heads]
        t_mat = [t_mat[hd] + both[hd][CHUNK:] for hd in heads]
    t_mat = [t_mat[hd] + _dot(t_mat[hd].astype(BF16), block_diag(power[hd])) for hd in heads]
    uw = [_dot(block_diag(t_mat[hd]),
               jnp.concatenate([v_beta[hd], k_beta[hd] * eg_b[hd]], axis=1).astype(BF16)) for hd in heads]
    q_dec = [qn[hd] * eg_b[hd] for hd in heads]

    state = [state_ref[hd] for hd in heads]
    o_inter = [[] for _ in heads]
    v_new_all = [[] for _ in heads]
    for c in range(MIX_CHUNKS):
        r0 = c * CHUNK
        for hd in heads:
            u_c = uw[hd][r0:r0 + CHUNK, :DN_HEAD_DIM]
            w_c = uw[hd][r0:r0 + CHUNK, DN_HEAD_DIM:]
            lhs = jnp.concatenate([w_c, q_dec[hd][r0:r0 + CHUNK]], axis=0).astype(BF16)
            ws = _dot(lhs, state[hd].astype(BF16))
            v_new = u_c - ws[:CHUNK]
            o_inter[hd].append(ws[CHUNK:])
            v_new_all[hd].append(v_new)
            v_pad = jnp.concatenate(
                [v_new if cc == c else jnp.zeros((CHUNK, DN_HEAD_DIM), F32) for cc in range(MIX_CHUNKS)],
                axis=0).astype(BF16)
            g_last = eg_b[hd][r0 + CHUNK - 1:r0 + CHUNK, :]
            state[hd] = state[hd] * g_last + _dot(kd_t[hd], v_pad)
    o_heads = []
    for hd in heads:
        lo = hd * DN_HEAD_DIM
        state_ref[hd] = state[hd]
        v_new_full = jnp.concatenate(v_new_all[hd], axis=0).astype(BF16)
        o = jnp.concatenate(o_inter[hd], axis=0) + _dot(qk[hd], v_new_full)
        o = o * lax.rsqrt(jnp.mean(o * o, axis=-1, keepdims=True) + RMS_EPS)
        o_heads.append(o * dng * _silu(z[:, lo:lo + DN_HEAD_DIM]))
    y_dn = _dot(jnp.concatenate(o_heads, axis=1).astype(BF16), dnproj_ref[...])

    merged = gate_dn * y_dn + gated_pool
    m = _dot(merged.astype(BF16), wout_ref[...])
    o_ref[0] = _layer_norm(ALPHA * h + m, lng_ref[...], lnb_ref[...])


def _mixer_ln(h3d, wqkv, wz, wba, wp, wgt, convw, prm, dng, dnproj, poolw, pscale, pproj, wout, lng, lnb):
    b, s, _ = h3d.shape
    operands = (wqkv, wz, wba, wp, wgt, convw, prm, dng, dnproj, poolw, pscale, pproj, wout, lng, lnb,
                *_mixer_masks())
    return pl.pallas_call(
        _mixer_kernel,
        grid=(b, s // MIX_ROWS),
        in_specs=[pl.BlockSpec((1, MIX_ROWS, D_MODEL), lambda i, j: (i, j, 0))]
        + [_resident(op.shape) for op in operands],
        out_specs=pl.BlockSpec((1, MIX_ROWS, D_MODEL), lambda i, j: (i, j, 0)),
        out_shape=jax.ShapeDtypeStruct(h3d.shape, F32),
        scratch_shapes=[
            pltpu.VMEM((DN_HEADS, DN_HEAD_DIM, DN_HEAD_DIM), F32),
            pltpu.VMEM((MIX_ROWS + SUBLANES, QKV_COLS), F32),
            pltpu.VMEM((MAX_WINDOW, POOL_WIDTH), F32),
        ],
        compiler_params=pltpu.CompilerParams(
            dimension_semantics=("arbitrary", "arbitrary"), vmem_limit_bytes=MIX_VMEM_LIMIT),
        name="mixer_ln",
    )(h3d, *operands)


def _row(v):
    return v.reshape(1, -1).astype(F32)


def kernel(x, ffn_pre_w_gate, ffn_pre_w_up, ffn_pre_w_down, norm_pre_g, norm_pre_b, mix_w_in, mix_conv_w, dn_a_log, dn_dt_bias, dn_norm_g, dn_w_proj, pool_w, pool_scale, pool_w_proj, mix_w_out, norm_mix_g, norm_mix_b, ffn_post_w_gate, ffn_post_w_up, ffn_post_w_down, norm_post_g, norm_post_b):
    bsz, seq, _ = x.shape
    h = x
    for l in range(DEPTH):
        h = _ffn_ln(h.reshape(bsz * seq, D_MODEL),
                    ffn_pre_w_gate[l].astype(BF16), ffn_pre_w_up[l].astype(BF16),
                    ffn_pre_w_down[l].astype(BF16), _row(norm_pre_g[l]), _row(norm_pre_b[l]),
                    "ffn_ln_pre")

        w_in = mix_w_in[l]
        c0 = QKV_COLS
        c1 = c0 + DN_WIDTH
        c2 = c1 + DN_HEADS
        c3 = c2 + DN_HEADS
        c4 = c3 + POOL_WIDTH
        wba = jnp.zeros((D_MODEL, 2 * LANES), F32)
        wba = wba.at[:, :DN_HEADS].set(w_in[:, c1:c2]).at[:, LANES:LANES + DN_HEADS].set(w_in[:, c2:c3])
        prm = jnp.zeros((SUBLANES, LANES), F32)
        prm = prm.at[0, :DN_HEADS].set(dn_a_log[l]).at[1, :DN_HEADS].set(dn_dt_bias[l])
        h = _mixer_ln(
            h.reshape(bsz, seq, D_MODEL),
            w_in[:, :c0].astype(BF16), w_in[:, c0:c1].astype(BF16), wba.astype(BF16),
            w_in[:, c3:c4].astype(BF16), w_in[:, c4:].astype(BF16),
            mix_conv_w[l].astype(F32), prm, _row(dn_norm_g[l]),
            dn_w_proj[l].astype(BF16), pool_w[l].astype(BF16), _row(pool_scale[l]),
            pool_w_proj[l].astype(BF16), mix_w_out[l].astype(BF16),
            _row(norm_mix_g[l]), _row(norm_mix_b[l]))

        h = _ffn_ln(h.reshape(bsz * seq, D_MODEL),
                    ffn_post_w_gate[l].astype(BF16), ffn_post_w_up[l].astype(BF16),
                    ffn_post_w_down[l].astype(BF16), _row(norm_post_g[l]), _row(norm_post_b[l]),
                    "ffn_ln_post")
    return h.reshape(bsz, seq, D_MODEL)
```

```python
import numpy as np

import jax
import jax.numpy as jnp
from jax import lax
from jax.experimental import pallas as pl
from jax.experimental.pallas import tpu as pltpu

F32 = jnp.float32
BF16 = jnp.bfloat16

D_MODEL = 1024
D_FF = 2816
DN_HEADS = 4
DN_HEAD_DIM = 128
DN_WIDTH = DN_HEADS * DN_HEAD_DIM
CONV_WIDTH = 4
CHUNK = 64
POOL_WINDOWS = (2, 4, 8, 16)
POOL_GROUPS = 4
POOL_GROUP_DIM = 128
POOL_WIDTH = POOL_GROUPS * POOL_GROUP_DIM
QKV_COLS = 3 * DN_WIDTH
DEPTH = 1
ALPHA = (2.0 * DEPTH) ** 0.25
LN_EPS = 1e-5
RMS_EPS = 1e-6

SUBLANES = 8
LANES = 128
MIB = 1024 * 1024

FFN_ROWS = 512
FFN_COLS = 256
MIX_ROWS = 256
MIX_CHUNKS = MIX_ROWS // CHUNK
MIX_BATCH = 2
MAX_WINDOW = max(POOL_WINDOWS)
FFN_VMEM_LIMIT = 48 * MIB
MIX_VMEM_LIMIT = 56 * MIB
SPLIT_COLS = 512


def _dot(a, b, **kw):
    return jnp.dot(a, b, preferred_element_type=F32, **kw)


def _dot_nt(a, b):
    return lax.dot_general(a, b, (((1,), (1,)), ((), ())), preferred_element_type=F32)


def _split_cols(w):
    k, n = w.shape
    return w.reshape(k, n // SPLIT_COLS, SPLIT_COLS).transpose(1, 0, 2).astype(BF16)


def _lane_split_dot(a, w_ref, rows=slice(None)):
    return jnp.concatenate([_dot(a, w_ref[i, rows, :]) for i in range(w_ref.shape[0])], axis=1)


def _layer_norm(y, g, b):
    mu = jnp.mean(y, axis=-1, keepdims=True)
    yc = y - mu
    var = jnp.mean(yc * yc, axis=-1, keepdims=True)
    return yc * lax.rsqrt(var + LN_EPS) * g + b


def _sigmoid(x):
    return 0.5 * jnp.tanh(0.5 * x) + 0.5


def _silu(x):
    half = 0.5 * x
    return half + half * jnp.tanh(half)


def _head_sums(sq, ones2):
    sq = sq.astype(BF16)
    width = ones2.shape[0]
    return jnp.concatenate(
        [_dot(sq[:, i * width:(i + 1) * width], ones2) for i in range(sq.shape[1] // width)], axis=1)


def _ffn_ln_kernel(x_ref, wg_ref, wu_ref, wd_ref, g_ref, b_ref, o_ref):
    x = x_ref[...]
    xb = x.astype(BF16)
    acc = jnp.zeros(x.shape, F32)
    for c in range(D_FF // FFN_COLS):
        cols = slice(c * FFN_COLS, (c + 1) * FFN_COLS)
        a = (_silu(_dot(xb, wg_ref[:, cols])) * _dot(xb, wu_ref[:, cols])).astype(BF16)
        acc = acc + _lane_split_dot(a, wd_ref, cols)
    o_ref[...] = _layer_norm(ALPHA * x + 0.5 * acc, g_ref[...], b_ref[...])


def _resident(shape):
    return pl.BlockSpec(shape, lambda *_: (0,) * len(shape), pipeline_mode=pl.Buffered(1))


def _ffn_ln(x2d, wg, wu, wd, g, b, name):
    t = x2d.shape[0]
    return pl.pallas_call(
        _ffn_ln_kernel,
        grid=(t // FFN_ROWS,),
        in_specs=[
            pl.BlockSpec((FFN_ROWS, D_MODEL), lambda i: (i, 0)),
            _resident((D_MODEL, D_FF)),
            _resident((D_MODEL, D_FF)),
            _resident(wd.shape),
            _resident((1, D_MODEL)),
            _resident((1, D_MODEL)),
        ],
        out_specs=pl.BlockSpec((FFN_ROWS, D_MODEL), lambda i: (i, 0)),
        out_shape=jax.ShapeDtypeStruct((t, D_MODEL), F32),
        compiler_params=pltpu.CompilerParams(
            dimension_semantics=("arbitrary",), vmem_limit_bytes=FFN_VMEM_LIMIT),
        name=name,
    )(x2d, wg, wu, wd, g, b)


def _mixer_masks():
    r = np.arange(MIX_ROWS)[:, None]
    c = np.arange(MIX_ROWS)[None, :]
    block_diag = (r // CHUNK == c // CHUNK)
    in_win = np.stack([(c <= r) & (c > r - w) for w in POOL_WINDOWS])
    rt = np.arange(MAX_WINDOW)[:, None]
    ct = np.arange(MAX_WINDOW)[None, :]
    in_tail = np.stack([(ct - MAX_WINDOW > rt - w) for w in POOL_WINDOWS])
    head_pair_ones = (r // DN_HEAD_DIM == c // DN_HEAD_DIM)
    as_bf16 = lambda m: jnp.asarray(m.astype(np.float32), dtype=BF16)
    return as_bf16(block_diag), as_bf16(in_win), as_bf16(in_tail), as_bf16(head_pair_ones)


def _mixer_kernel(h_ref, wqkv_ref, wz_ref, wba_ref, wp_ref, wgt_ref, convw_ref, prm_ref, dng_ref,
                  dnproj_ref, poolw_ref, pscale_ref, pproj_ref, wout_ref, lng_ref, lnb_ref,
                  bdmask_ref, inwin_ref, intail_ref, ones2_ref,
                  o_ref, state_ref, xtail_ref, ptail_ref):
    ts = MIX_ROWS
    nb = MIX_BATCH
    tile = pl.program_id(1)
    seqs = range(nb)
    rows_of = lambda s: slice(s * ts, (s + 1) * ts)

    @pl.when(tile == 0)
    def _():
        state_ref[...] = jnp.zeros(state_ref.shape, F32)
        xtail_ref[...] = jnp.zeros(xtail_ref.shape, F32)
        ptail_ref[...] = jnp.zeros(ptail_ref.shape, F32)

    h = h_ref[...].reshape(nb * ts, D_MODEL)
    hb = h.astype(BF16)

    ba_t = _dot_nt(wba_ref[...], hb)
    beta_t = _sigmoid(ba_t[0:SUBLANES])
    a_shift = ba_t[SUBLANES:] + prm_ref[1]
    softplus = jnp.maximum(a_shift, 0.0) + jnp.log1p(jnp.exp(-jnp.abs(a_shift)))
    gr = -jnp.exp(prm_ref[0]) * softplus
    lane_in_chunk = lax.broadcasted_iota(jnp.int32, (SUBLANES, nb * ts), 1) % CHUNK
    shift = 1
    while shift < CHUNK:
        rolled = jnp.concatenate(
            [pltpu.roll(gr[:, i * LANES:(i + 1) * LANES], shift, 1) for i in range(nb * ts // LANES)],
            axis=1)
        gr = gr + jnp.where(lane_in_chunk >= shift, rolled, 0.0)
        shift *= 2

    def column_broadcast(row):
        return jnp.broadcast_to(row, (DN_HEAD_DIM, ts)).T

    def gate_group(j):
        return _sigmoid(_dot(hb, wgt_ref[j]))

    ones2 = ones2_ref[...]
    qkv = []
    gate_sig = []
    for part in range(3):
        cols = slice(part * DN_WIDTH, (part + 1) * DN_WIDTH)
        raw = _dot(hb, wqkv_ref[:, cols])
        cw = convw_ref[:, cols]
        conv = []
        for s in seqs:
            raw_s = raw[rows_of(s)]
            ext = jnp.concatenate([xtail_ref[s, :, cols], raw_s], axis=0)
            xtail_ref[s, :, cols] = raw_s[ts - SUBLANES:, :]
            y = pltpu.roll(ext, CONV_WIDTH - 1, 0)[SUBLANES:] * cw[0:1]
            for tap in range(1, CONV_WIDTH - 1):
                y = y + pltpu.roll(ext, CONV_WIDTH - 1 - tap, 0)[SUBLANES:] * cw[tap:tap + 1]
            conv.append(y + raw_s * cw[CONV_WIDTH - 1:CONV_WIDTH])
        qkv.append(_silu(jnp.concatenate(conv, axis=0)))
        gate_sig.append(gate_group(part))
    q_all = qkv[0] * (lax.rsqrt(_head_sums(qkv[0] * qkv[0], ones2) + RMS_EPS) * (DN_HEAD_DIM ** -0.5))
    k_all = qkv[1] * lax.rsqrt(_head_sums(qkv[1] * qkv[1], ones2) + RMS_EPS)
    v_all = qkv[2]

    p = _dot(hb, wp_ref[...])
    t_abs = tile * ts + lax.broadcasted_iota(jnp.int32, (ts, POOL_GROUP_DIM), 0)
    pooled = []
    for s in seqs:
        p_s = p[rows_of(s)]
        tail = ptail_ref[s]
        ptail_ref[s] = p_s[ts - MAX_WINDOW:, :]
        pooled_s = []
        for gi, win in enumerate(POOL_WINDOWS):
            lo = gi * POOL_GROUP_DIM
            pg = p_s[:, lo:lo + POOL_GROUP_DIM]
            wsum = _dot(inwin_ref[gi], pg.astype(BF16))
            top = wsum[:MAX_WINDOW] + _dot(intail_ref[gi], tail[:, lo:lo + POOL_GROUP_DIM].astype(BF16))
            wsum = jnp.concatenate([top, wsum[MAX_WINDOW:]], axis=0)
            count = jnp.minimum(t_abs + 1, win).astype(F32)
            pooled_s.append(wsum / count - pg)
        pooled.append(jnp.concatenate(pooled_s, axis=1))
    pooled = jnp.concatenate(pooled, axis=0).astype(BF16)
    pooled = jnp.concatenate(
        [_dot(pooled[:, gi * POOL_GROUP_DIM:(gi + 1) * POOL_GROUP_DIM], poolw_ref[gi])
         for gi in range(POOL_GROUPS)], axis=1) * pscale_ref[...]
    y_pool = _lane_split_dot(pooled.astype(BF16), pproj_ref)
    z = _dot(hb, wz_ref[...])
    gate_sig.append(gate_group(3))

    bdmask = bdmask_ref[...]
    prow = lax.broadcasted_iota(jnp.int32, (CHUNK, ts), 0)
    plane = lax.broadcasted_iota(jnp.int32, (CHUNK, ts), 1)
    pcol = plane % CHUNK
    in_block = [plane // CHUNK == c for c in range(MIX_CHUNKS)]
    causal_p = prow >= pcol
    strict_p = prow > pcol
    eye_p = (prow == pcol).astype(F32)
    low_lanes = lax.broadcasted_iota(jnp.int32, (CHUNK, LANES), 1) < CHUNK

    def pack(full):
        out = full[0:CHUNK]
        for c in range(1, MIX_CHUNKS):
            out = jnp.where(in_block[c], full[c * CHUNK:(c + 1) * CHUNK], out)
        return out

    def block_diag(packed):
        return jnp.concatenate([packed.astype(BF16)] * MIX_CHUNKS, axis=0) * bdmask

    chains = [(s, hd) for s in seqs for hd in range(DN_HEADS)]
    ids = range(len(chains))
    qn, kn_b, k_beta, v_beta, eg_b, kd_t, decay_p = [], [], [], [], [], [], []
    for s, hd in chains:
        lanes = slice(hd * DN_HEAD_DIM, (hd + 1) * DN_HEAD_DIM)
        qn.append(q_all[rows_of(s), lanes])
        kn = k_all[rows_of(s), lanes]
        v = v_all[rows_of(s), lanes]
        beta_b = column_broadcast(beta_t[hd:hd + 1, rows_of(s)])
        gr_row = gr[hd:hd + 1, rows_of(s)]
        gc_b = column_broadcast(gr_row)
        gc_p = jnp.concatenate(
            [jnp.where(low_lanes, gc_b[2 * i * CHUNK:(2 * i + 1) * CHUNK], gc_b[(2 * i + 1) * CHUNK:(2 * i + 2) * CHUNK])
             for i in range(MIX_CHUNKS // 2)], axis=1)
        decay_p.append(jnp.exp(jnp.where(causal_p, gc_p - gr_row, -jnp.inf)))
        eg_b.append(jnp.exp(gc_b))
        gl_b = jnp.concatenate(
            [jnp.broadcast_to(gc_b[c * CHUNK + CHUNK - 1:(c + 1) * CHUNK, :], (CHUNK, DN_HEAD_DIM))
             for c in range(MIX_CHUNKS)], axis=0)
        k_beta.append(kn * beta_b)
        v_beta.append(v * beta_b)
        kn_b.append(kn.astype(BF16))
        kd_t.append((kn * jnp.exp(gl_b - gc_b)).T.astype(BF16))

    gram = [_dot_nt(jnp.concatenate([k_beta[i], qn[i]], axis=0).astype(BF16), kn_b[i]) for i in ids]
    lower = [pack(gram[i][:ts]) * jnp.where(strict_p, decay_p[i], 0.0) for i in ids]
    qk = [block_diag(pack(gram[i][ts:]) * decay_p[i]) for i in ids]
    t_mat = [eye_p - lower[i] for i in ids]
    power = [_dot(lower[i].astype(BF16), block_diag(lower[i])) for i in ids]
    for _ in range(4):
        both = [_dot(jnp.concatenate([power[i], t_mat[i]], axis=0).astype(BF16), block_diag(power[i]))
                for i in ids]
        power = [both[i][:CHUNK] for i in ids]
        t_mat = [t_mat[i] + both[i][CHUNK:] for i in ids]
    t_mat = [t_mat[i] + _dot(t_mat[i].astype(BF16), block_diag(power[i])) for i in ids]
    uw = [_dot(block_diag(t_mat[i]),
               jnp.concatenate([v_beta[i], k_beta[i] * eg_b[i]], axis=1).astype(BF16)) for i in ids]
    q_dec = [qn[i] * eg_b[i] for i in ids]

    state = [state_ref[s, hd] for s, hd in chains]
    o_inter = [[] for _ in ids]
    v_new_all = [[] for _ in ids]
    for c in range(MIX_CHUNKS):
        r0 = c * CHUNK
        for i in ids:
            u_c = uw[i][r0:r0 + CHUNK, :DN_HEAD_DIM]
            w_c = uw[i][r0:r0 + CHUNK, DN_HEAD_DIM:]
            lhs = jnp.concatenate([w_c, q_dec[i][r0:r0 + CHUNK]], axis=0).astype(BF16)
            ws = _dot(lhs, state[i].astype(BF16))
            v_new = u_c - ws[:CHUNK]
            o_inter[i].append(ws[CHUNK:])
            v_new_all[i].append(v_new)
            v_pad = jnp.concatenate(
                [v_new if cc == c else jnp.zeros((CHUNK, DN_HEAD_DIM), F32) for cc in range(MIX_CHUNKS)],
                axis=0).astype(BF16)
            g_last = eg_b[i][r0 + CHUNK - 1:r0 + CHUNK, :]
            state[i] = state[i] * g_last + _dot(kd_t[i], v_pad)
    o_chain = []
    for i, (s, hd) in enumerate(chains):
        state_ref[s, hd] = state[i]
        v_new_full = jnp.concatenate(v_new_all[i], axis=0).astype(BF16)
        o_chain.append(jnp.concatenate(o_inter[i], axis=0) + _dot(qk[i], v_new_full))
    o_all = jnp.concatenate(
        [jnp.concatenate(o_chain[s * DN_HEADS:(s + 1) * DN_HEADS], axis=1) for s in seqs], axis=0)
    o_all = o_all * lax.rsqrt(_head_sums(o_all * o_all, ones2) * (1.0 / DN_HEAD_DIM) + RMS_EPS)
    o_all = o_all * dng_ref[...] * _silu(z)
    y_dn = _dot(o_all.astype(BF16), dnproj_ref[...])

    merged = (jnp.concatenate(gate_sig[:2], axis=1) * y_dn
              + jnp.concatenate(gate_sig[2:], axis=1) * y_pool)
    m = _lane_split_dot(merged.astype(BF16), wout_ref)
    o_ref[...] = _layer_norm(ALPHA * h + m, lng_ref[...], lnb_ref[...]).reshape(nb, ts, D_MODEL)


def _mixer_ln(h3d, wqkv, wz, wba, wp, wgt, convw, prm, dng, dnproj, poolw, pscale, pproj, wout, lng, lnb):
    b, s, _ = h3d.shape
    operands = (wqkv, wz, wba, wp, wgt, convw, prm, dng, dnproj, poolw, pscale, pproj, wout, lng, lnb,
                *_mixer_masks())
    block = (MIX_BATCH, MIX_ROWS, D_MODEL)
    return pl.pallas_call(
        _mixer_kernel,
        grid=(b // MIX_BATCH, s // MIX_ROWS),
        in_specs=[pl.BlockSpec(block, lambda i, j: (i, j, 0))] + [_resident(op.shape) for op in operands],
        out_specs=pl.BlockSpec(block, lambda i, j: (i, j, 0)),
        out_shape=jax.ShapeDtypeStruct(h3d.shape, F32),
        scratch_shapes=[
            pltpu.VMEM((MIX_BATCH, DN_HEADS, DN_HEAD_DIM, DN_HEAD_DIM), F32),
            pltpu.VMEM((MIX_BATCH, SUBLANES, QKV_COLS), F32),
            pltpu.VMEM((MIX_BATCH, MAX_WINDOW, POOL_WIDTH), F32),
        ],
        compiler_params=pltpu.CompilerParams(
            dimension_semantics=("arbitrary", "arbitrary"), vmem_limit_bytes=MIX_VMEM_LIMIT),
        name="mixer_ln",
    )(h3d, *operands)


def _row(v):
    return v.reshape(1, -1).astype(F32)


def kernel(x, ffn_pre_w_gate, ffn_pre_w_up, ffn_pre_w_down, norm_pre_g, norm_pre_b, mix_w_in, mix_conv_w, dn_a_log, dn_dt_bias, dn_norm_g, dn_w_proj, pool_w, pool_scale, pool_w_proj, mix_w_out, norm_mix_g, norm_mix_b, ffn_post_w_gate, ffn_post_w_up, ffn_post_w_down, norm_post_g, norm_post_b):
    bsz, seq, _ = x.shape
    h = x
    for l in range(DEPTH):
        h = _ffn_ln(h.reshape(bsz * seq, D_MODEL),
                    ffn_pre_w_gate[l].astype(BF16), ffn_pre_w_up[l].astype(BF16),
                    _split_cols(ffn_pre_w_down[l]), _row(norm_pre_g[l]), _row(norm_pre_b[l]),
                    "ffn_ln_pre")

        w_in = mix_w_in[l]
        c0 = QKV_COLS
        c1 = c0 + DN_WIDTH
        c2 = c1 + DN_HEADS
        c3 = c2 + DN_HEADS
        c4 = c3 + POOL_WIDTH
        wba_t = jnp.zeros((2 * SUBLANES, D_MODEL), F32)
        wba_t = wba_t.at[:DN_HEADS].set(w_in[:, c1:c2].T).at[SUBLANES:SUBLANES + DN_HEADS].set(w_in[:, c2:c3].T)
        prm = jnp.zeros((2, SUBLANES, MIX_BATCH * MIX_ROWS), F32)
        prm = prm.at[0, :DN_HEADS].set(jnp.broadcast_to(dn_a_log[l][:, None], (DN_HEADS, MIX_BATCH * MIX_ROWS)))
        prm = prm.at[1, :DN_HEADS].set(jnp.broadcast_to(dn_dt_bias[l][:, None], (DN_HEADS, MIX_BATCH * MIX_ROWS)))
        h = _mixer_ln(
            h.reshape(bsz, seq, D_MODEL),
            w_in[:, :c0].astype(BF16), w_in[:, c0:c1].astype(BF16), wba_t.astype(BF16),
            w_in[:, c3:c4].astype(BF16), _split_cols(w_in[:, c4:]),
            mix_conv_w[l].astype(F32), prm, _row(jnp.tile(dn_norm_g[l], DN_HEADS)),
            dn_w_proj[l].astype(BF16), pool_w[l].astype(BF16), _row(pool_scale[l]),
            _split_cols(pool_w_proj[l]), _split_cols(mix_w_out[l]),
            _row(norm_mix_g[l]), _row(norm_mix_b[l]))

        h = _ffn_ln(h.reshape(bsz * seq, D_MODEL),
                    ffn_post_w_gate[l].astype(BF16), ffn_post_w_up[l].astype(BF16),
                    _split_cols(ffn_post_w_down[l]), _row(norm_post_g[l]), _row(norm_post_b[l]),
                    "ffn_ln_post")
    return h.reshape(bsz, seq, D_MODEL)
```

```python
import numpy as np

import jax
import jax.numpy as jnp
from jax import lax
from jax.experimental import pallas as pl
from jax.experimental.pallas import tpu as pltpu

F32 = jnp.float32
BF16 = jnp.bfloat16

D_MODEL = 1024
D_FF = 2816
DN_HEADS = 4
DN_HEAD_DIM = 128
DN_WIDTH = DN_HEADS * DN_HEAD_DIM
CONV_WIDTH = 4
CHUNK = 64
POOL_WINDOWS = (2, 4, 8, 16)
POOL_GROUPS = 4
POOL_GROUP_DIM = 128
POOL_WIDTH = POOL_GROUPS * POOL_GROUP_DIM
QKV_COLS = 3 * DN_WIDTH
DEPTH = 1
ALPHA = (2.0 * DEPTH) ** 0.25
LN_EPS = 1e-5
RMS_EPS = 1e-6

SUBLANES = 8
LANES = 128
MIB = 1024 * 1024

FFN_ROWS = 512
FFN_COLS = 256
MIX_ROWS = 256
MIX_CHUNKS = MIX_ROWS // CHUNK
MIX_BATCH = 2
MIX_GROUP = 2
MIX_LAG = 6
MAX_WINDOW = max(POOL_WINDOWS)
FFN_VMEM_LIMIT = 48 * MIB
MIX_VMEM_LIMIT = 56 * MIB
SPLIT_COLS = 512


def _dot(a, b, **kw):
    return jnp.dot(a, b, preferred_element_type=F32, **kw)


def _dot_nt(a, b):
    return lax.dot_general(a, b, (((1,), (1,)), ((), ())), preferred_element_type=F32)


def _split_cols(w):
    k, n = w.shape
    return w.reshape(k, n // SPLIT_COLS, SPLIT_COLS).transpose(1, 0, 2).astype(BF16)


def _lane_split_dot(a, w_ref, rows=slice(None)):
    return jnp.concatenate([_dot(a, w_ref[i, rows, :]) for i in range(w_ref.shape[0])], axis=1)


def _layer_norm(y, g, b):
    mu = jnp.mean(y, axis=-1, keepdims=True)
    yc = y - mu
    var = jnp.mean(yc * yc, axis=-1, keepdims=True)
    return yc * lax.rsqrt(var + LN_EPS) * g + b


def _sigmoid(x):
    return 0.5 * jnp.tanh(0.5 * x) + 0.5


def _silu(x):
    half = 0.5 * x
    return half + half * jnp.tanh(half)


def _head_sums(sq, ones2):
    sq = sq.astype(BF16)
    width = ones2.shape[0]
    return jnp.concatenate(
        [_dot(sq[:, i * width:(i + 1) * width], ones2) for i in range(sq.shape[1] // width)], axis=1)


def _ffn_ln_kernel(x_ref, wg_ref, wu_ref, wd_ref, g_ref, b_ref, o_ref):
    x = x_ref[...]
    xb = x.astype(BF16)
    acc = jnp.zeros(x.shape, F32)
    for c in range(D_FF // FFN_COLS):
        cols = slice(c * FFN_COLS, (c + 1) * FFN_COLS)
        a = (_silu(_dot(xb, wg_ref[:, cols])) * _dot(xb, wu_ref[:, cols])).astype(BF16)
        acc = acc + _lane_split_dot(a, wd_ref, cols)
    o_ref[...] = _layer_norm(ALPHA * x + 0.5 * acc, g_ref[...], b_ref[...])


def _resident(shape):
    return pl.BlockSpec(shape, lambda *_: (0,) * len(shape), pipeline_mode=pl.Buffered(1))


def _ffn_ln(x2d, wg, wu, wd, g, b, name):
    t = x2d.shape[0]
    return pl.pallas_call(
        _ffn_ln_kernel,
        grid=(t // FFN_ROWS,),
        in_specs=[
            pl.BlockSpec((FFN_ROWS, D_MODEL), lambda i: (i, 0)),
            _resident((D_MODEL, D_FF)),
            _resident((D_MODEL, D_FF)),
            _resident(wd.shape),
            _resident((1, D_MODEL)),
            _resident((1, D_MODEL)),
        ],
        out_specs=pl.BlockSpec((FFN_ROWS, D_MODEL), lambda i: (i, 0)),
        out_shape=jax.ShapeDtypeStruct((t, D_MODEL), F32),
        compiler_params=pltpu.CompilerParams(
            dimension_semantics=("arbitrary",), vmem_limit_bytes=FFN_VMEM_LIMIT),
        name=name,
    )(x2d, wg, wu, wd, g, b)


def _mixer_masks():
    r = np.arange(MIX_ROWS)[:, None]
    c = np.arange(MIX_ROWS)[None, :]
    block_diag = (r // CHUNK == c // CHUNK)
    in_win = np.stack([(c <= r) & (c > r - w) for w in POOL_WINDOWS])
    rt = np.arange(MAX_WINDOW)[:, None]
    ct = np.arange(MAX_WINDOW)[None, :]
    in_tail = np.stack([(ct - MAX_WINDOW > rt - w) for w in POOL_WINDOWS])
    head_pair_ones = (r // DN_HEAD_DIM == c // DN_HEAD_DIM)
    as_bf16 = lambda m: jnp.asarray(m.astype(np.float32), dtype=BF16)
    return as_bf16(block_diag), as_bf16(in_win), as_bf16(in_tail), as_bf16(head_pair_ones)


def _run_skewed(pipelines, lag):
    pending = list(pipelines)
    live = []
    step = 0
    while pending or live:
        if pending and step % lag == 0:
            live.append(pending.pop(0))
        for p in list(live):
            try:
                next(p)
            except StopIteration:
                live.remove(p)
        step += 1


def _mixer_kernel(h_ref, wqkv_ref, wz_ref, wba_ref, wp_ref, wgt_ref, convw_ref, prm_ref, dng_ref,
                  dnproj_ref, poolw_ref, pscale_ref, pproj_ref, wout_ref, lng_ref, lnb_ref,
                  bdmask_ref, inwin_ref, intail_ref, ones2_ref,
                  o_ref, state_ref, xtail_ref, ptail_ref):
    ts = MIX_ROWS
    tile = pl.program_id(1)

    @pl.when(tile == 0)
    def _():
        state_ref[...] = jnp.zeros(state_ref.shape, F32)
        xtail_ref[...] = jnp.zeros(xtail_ref.shape, F32)
        ptail_ref[...] = jnp.zeros(ptail_ref.shape, F32)

    ones2 = ones2_ref[...]
    bdmask = bdmask_ref[...]
    prow = lax.broadcasted_iota(jnp.int32, (CHUNK, ts), 0)
    plane = lax.broadcasted_iota(jnp.int32, (CHUNK, ts), 1)
    pcol = plane % CHUNK
    in_block = [plane // CHUNK == c for c in range(MIX_CHUNKS)]
    causal_p = prow >= pcol
    strict_p = prow > pcol
    eye_p = (prow == pcol).astype(F32)
    low_lanes = lax.broadcasted_iota(jnp.int32, (CHUNK, LANES), 1) < CHUNK
    t_abs = tile * ts + lax.broadcasted_iota(jnp.int32, (ts, POOL_GROUP_DIM), 0)

    def column_broadcast(row):
        return jnp.broadcast_to(row, (DN_HEAD_DIM, ts)).T

    def pack(full):
        out = full[0:CHUNK]
        for c in range(1, MIX_CHUNKS):
            out = jnp.where(in_block[c], full[c * CHUNK:(c + 1) * CHUNK], out)
        return out

    def block_diag(packed):
        return jnp.concatenate([packed.astype(BF16)] * MIX_CHUNKS, axis=0) * bdmask

    def group_stages(first):
        ng = MIX_GROUP
        seqs = range(ng)
        rows_of = lambda s: slice(s * ts, (s + 1) * ts)
        h = h_ref[first:first + ng].reshape(ng * ts, D_MODEL)
        hb = h.astype(BF16)

        ba_t = _dot_nt(wba_ref[...], hb)
        beta_t = _sigmoid(ba_t[0:SUBLANES])
        a_shift = ba_t[SUBLANES:] + prm_ref[1]
        softplus = jnp.maximum(a_shift, 0.0) + jnp.log1p(jnp.exp(-jnp.abs(a_shift)))
        gr = -jnp.exp(prm_ref[0]) * softplus
        lane_in_chunk = lax.broadcasted_iota(jnp.int32, (SUBLANES, ng * ts), 1) % CHUNK
        shift = 1
        while shift < CHUNK:
            rolled = jnp.concatenate(
                [pltpu.roll(gr[:, i * LANES:(i + 1) * LANES], shift, 1) for i in range(ng * ts // LANES)],
                axis=1)
            gr = gr + jnp.where(lane_in_chunk >= shift, rolled, 0.0)
            shift *= 2
        yield

        def gate_group(j):
            return _sigmoid(_dot(hb, wgt_ref[j]))

        qkv = []
        gate_sig = []
        for part in range(3):
            cols = slice(part * DN_WIDTH, (part + 1) * DN_WIDTH)
            raw = _dot(hb, wqkv_ref[:, cols])
            cw = convw_ref[:, cols]
            conv = []
            for s in seqs:
                raw_s = raw[rows_of(s)]
                ext = jnp.concatenate([xtail_ref[first + s, :, cols], raw_s], axis=0)
                xtail_ref[first + s, :, cols] = raw_s[ts - SUBLANES:, :]
                y = pltpu.roll(ext, CONV_WIDTH - 1, 0)[SUBLANES:] * cw[0:1]
                for tap in range(1, CONV_WIDTH - 1):
                    y = y + pltpu.roll(ext, CONV_WIDTH - 1 - tap, 0)[SUBLANES:] * cw[tap:tap + 1]
                conv.append(y + raw_s * cw[CONV_WIDTH - 1:CONV_WIDTH])
            qkv.append(_silu(jnp.concatenate(conv, axis=0)))
            gate_sig.append(gate_group(part))
            yield
        q_all = qkv[0] * (lax.rsqrt(_head_sums(qkv[0] * qkv[0], ones2) + RMS_EPS) * (DN_HEAD_DIM ** -0.5))
        k_all = qkv[1] * lax.rsqrt(_head_sums(qkv[1] * qkv[1], ones2) + RMS_EPS)
        v_all = qkv[2]

        p = _dot(hb, wp_ref[...])
        pooled = []
        for s in seqs:
            p_s = p[rows_of(s)]
            tail = ptail_ref[first + s]
            ptail_ref[first + s] = p_s[ts - MAX_WINDOW:, :]
            pooled_s = []
            for gi, win in enumerate(POOL_WINDOWS):
                lo = gi * POOL_GROUP_DIM
                pg = p_s[:, lo:lo + POOL_GROUP_DIM]
                wsum = _dot(inwin_ref[gi], pg.astype(BF16))
                top = wsum[:MAX_WINDOW] + _dot(intail_ref[gi], tail[:, lo:lo + POOL_GROUP_DIM].astype(BF16))
                wsum = jnp.concatenate([top, wsum[MAX_WINDOW:]], axis=0)
                count = jnp.minimum(t_abs + 1, win).astype(F32)
                pooled_s.append(wsum / count - pg)
            pooled.append(jnp.concatenate(pooled_s, axis=1))
        pooled = jnp.concatenate(pooled, axis=0).astype(BF16)
        pooled = jnp.concatenate(
            [_dot(pooled[:, gi * POOL_GROUP_DIM:(gi + 1) * POOL_GROUP_DIM], poolw_ref[gi])
             for gi in range(POOL_GROUPS)], axis=1) * pscale_ref[...]
        y_pool = _lane_split_dot(pooled.astype(BF16), pproj_ref)
        z = _dot(hb, wz_ref[...])
        gate_sig.append(gate_group(3))
        yield

        chains = [(s, hd) for s in seqs for hd in range(DN_HEADS)]
        ids = range(len(chains))
        qn, kn_b, k_beta, v_beta, eg_b, kd_t, decay_p = [], [], [], [], [], [], []
        for s, hd in chains:
            lanes = slice(hd * DN_HEAD_DIM, (hd + 1) * DN_HEAD_DIM)
            qn.append(q_all[rows_of(s), lanes])
            kn = k_all[rows_of(s), lanes]
            v = v_all[rows_of(s), lanes]
            beta_b = column_broadcast(beta_t[hd:hd + 1, rows_of(s)])
            gr_row = gr[hd:hd + 1, rows_of(s)]
            gc_b = column_broadcast(gr_row)
            gc_p = jnp.concatenate(
                [jnp.where(low_lanes, gc_b[2 * i * CHUNK:(2 * i + 1) * CHUNK],
                           gc_b[(2 * i + 1) * CHUNK:(2 * i + 2) * CHUNK])
                 for i in range(MIX_CHUNKS // 2)], axis=1)
            decay_p.append(jnp.exp(jnp.where(causal_p, gc_p - gr_row, -jnp.inf)))
            eg_b.append(jnp.exp(gc_b))
            gl_b = jnp.concatenate(
                [jnp.broadcast_to(gc_b[c * CHUNK + CHUNK - 1:(c + 1) * CHUNK, :], (CHUNK, DN_HEAD_DIM))
                 for c in range(MIX_CHUNKS)], axis=0)
            k_beta.append(kn * beta_b)
            v_beta.append(v * beta_b)
            kn_b.append(kn.astype(BF16))
            kd_t.append((kn * jnp.exp(gl_b - gc_b)).T.astype(BF16))
        gram = [_dot_nt(jnp.concatenate([k_beta[i], qn[i]], axis=0).astype(BF16), kn_b[i]) for i in ids]
        yield
        lower = [pack(gram[i][:ts]) * jnp.where(strict_p, decay_p[i], 0.0) for i in ids]
        qk = [block_diag(pack(gram[i][ts:]) * decay_p[i]) for i in ids]
        t_mat = [eye_p - lower[i] for i in ids]
        power = [_dot(lower[i].astype(BF16), block_diag(lower[i])) for i in ids]
        yield
        for _ in range(4):
            both = [_dot(jnp.concatenate([power[i], t_mat[i]], axis=0).astype(BF16), block_diag(power[i]))
                    for i in ids]
            power = [both[i][:CHUNK] for i in ids]
            t_mat = [t_mat[i] + both[i][CHUNK:] for i in ids]
            yield
        t_mat = [t_mat[i] + _dot(t_mat[i].astype(BF16), block_diag(power[i])) for i in ids]
        yield
        uw = [_dot(block_diag(t_mat[i]),
                   jnp.concatenate([v_beta[i], k_beta[i] * eg_b[i]], axis=1).astype(BF16)) for i in ids]
        q_dec = [qn[i] * eg_b[i] for i in ids]
        state = [state_ref[first + s, hd] for s, hd in chains]
        yield
        kd_uw = []
        for i in ids:
            per_chunk = []
            for c in range(MIX_CHUNKS):
                uw_pad = jnp.concatenate(
                    [uw[i][cc * CHUNK:(cc + 1) * CHUNK] if cc == c else jnp.zeros((CHUNK, 2 * DN_HEAD_DIM), F32)
                     for cc in range(MIX_CHUNKS)], axis=0).astype(BF16)
                per_chunk.append(_dot(kd_t[i], uw_pad))
            kd_uw.append(per_chunk)
        yield
        o_inter = [[] for _ in ids]
        v_new_all = [[] for _ in ids]
        for c in range(MIX_CHUNKS):
            r0 = c * CHUNK
            for i in ids:
                u_c = uw[i][r0:r0 + CHUNK, :DN_HEAD_DIM]
                w_c = uw[i][r0:r0 + CHUNK, DN_HEAD_DIM:]
                lhs = jnp.concatenate(
                    [kd_uw[i][c][:, DN_HEAD_DIM:], w_c, q_dec[i][r0:r0 + CHUNK]], axis=0).astype(BF16)
                prod = _dot(lhs, state[i].astype(BF16))
                v_new_all[i].append(u_c - prod[DN_HEAD_DIM:DN_HEAD_DIM + CHUNK])
                o_inter[i].append(prod[DN_HEAD_DIM + CHUNK:])
                g_last = eg_b[i][r0 + CHUNK - 1:r0 + CHUNK, :]
                state[i] = state[i] * g_last - prod[:DN_HEAD_DIM] + kd_uw[i][c][:, :DN_HEAD_DIM]
            yield
        o_chain = []
        for i, (s, hd) in enumerate(chains):
            state_ref[first + s, hd] = state[i]
            v_new_full = jnp.concatenate(v_new_all[i], axis=0).astype(BF16)
            o_chain.append(jnp.concatenate(o_inter[i], axis=0) + _dot(qk[i], v_new_full))
        o_all = jnp.concatenate(
            [jnp.concatenate(o_chain[s * DN_HEADS:(s + 1) * DN_HEADS], axis=1) for s in seqs], axis=0)
        o_all = o_all * lax.rsqrt(_head_sums(o_all * o_all, ones2) * (1.0 / DN_HEAD_DIM) + RMS_EPS)
        o_all = o_all * dng_ref[...] * _silu(z)
        y_dn = _dot(o_all.astype(BF16), dnproj_ref[...])
        yield

        merged = (jnp.concatenate(gate_sig[:2], axis=1) * y_dn
                  + jnp.concatenate(gate_sig[2:], axis=1) * y_pool)
        m = _lane_split_dot(merged.astype(BF16), wout_ref)
        out = _layer_norm(ALPHA * h + m, lng_ref[...], lnb_ref[...])
        o_ref[first:first + ng] = out.reshape(ng, ts, D_MODEL)
        yield

    _run_skewed([group_stages(first) for first in range(0, MIX_BATCH, MIX_GROUP)], MIX_LAG)


def _mixer_ln(h3d, wqkv, wz, wba, wp, wgt, convw, prm, dng, dnproj, poolw, pscale, pproj, wout, lng, lnb):
    b, s, _ = h3d.shape
    operands = (wqkv, wz, wba, wp, wgt, convw, prm, dng, dnproj, poolw, pscale, pproj, wout, lng, lnb,
                *_mixer_masks())
    block = (MIX_BATCH, MIX_ROWS, D_MODEL)
    return pl.pallas_call(
        _mixer_kernel,
        grid=(b // MIX_BATCH, s // MIX_ROWS),
        in_specs=[pl.BlockSpec(block, lambda i, j: (i, j, 0))] + [_resident(op.shape) for op in operands],
        out_specs=pl.BlockSpec(block, lambda i, j: (i, j, 0)),
        out_shape=jax.ShapeDtypeStruct(h3d.shape, F32),
        scratch_shapes=[
            pltpu.VMEM((MIX_BATCH, DN_HEADS, DN_HEAD_DIM, DN_HEAD_DIM), F32),
            pltpu.VMEM((MIX_BATCH, SUBLANES, QKV_COLS), F32),
            pltpu.VMEM((MIX_BATCH, MAX_WINDOW, POOL_WIDTH), F32),
        ],
        compiler_params=pltpu.CompilerParams(
            dimension_semantics=("arbitrary", "arbitrary"), vmem_limit_bytes=MIX_VMEM_LIMIT),
        name="mixer_ln",
    )(h3d, *operands)


def _row(v):
    return v.reshape(1, -1).astype(F32)


def kernel(x, ffn_pre_w_gate, ffn_pre_w_up, ffn_pre_w_down, norm_pre_g, norm_pre_b, mix_w_in, mix_conv_w, dn_a_log, dn_dt_bias, dn_norm_g, dn_w_proj, pool_w, pool_scale, pool_w_proj, mix_w_out, norm_mix_g, norm_mix_b, ffn_post_w_gate, ffn_post_w_up, ffn_post_w_down, norm_post_g, norm_post_b):
    bsz, seq, _ = x.shape
    h = x
    for l in range(DEPTH):
        h = _ffn_ln(h.reshape(bsz * seq, D_MODEL),
                    ffn_pre_w_gate[l].astype(BF16), ffn_pre_w_up[l].astype(BF16),
                    _split_cols(ffn_pre_w_down[l]), _row(norm_pre_g[l]), _row(norm_pre_b[l]),
                    "ffn_ln_pre")

        w_in = mix_w_in[l]
        c0 = QKV_COLS
        c1 = c0 + DN_WIDTH
        c2 = c1 + DN_HEADS
        c3 = c2 + DN_HEADS
        c4 = c3 + POOL_WIDTH
        wba_t = jnp.zeros((2 * SUBLANES, D_MODEL), F32)
        wba_t = wba_t.at[:DN_HEADS].set(w_in[:, c1:c2].T).at[SUBLANES:SUBLANES + DN_HEADS].set(w_in[:, c2:c3].T)
        prm = jnp.zeros((2, SUBLANES, MIX_GROUP * MIX_ROWS), F32)
        prm = prm.at[0, :DN_HEADS].set(jnp.broadcast_to(dn_a_log[l][:, None], (DN_HEADS, MIX_GROUP * MIX_ROWS)))
        prm = prm.at[1, :DN_HEADS].set(jnp.broadcast_to(dn_dt_bias[l][:, None], (DN_HEADS, MIX_GROUP * MIX_ROWS)))
        h = _mixer_ln(
            h.reshape(bsz, seq, D_MODEL),
            w_in[:, :c0].astype(BF16), w_in[:, c0:c1].astype(BF16), wba_t.astype(BF16),
            w_in[:, c3:c4].astype(BF16), _split_cols(w_in[:, c4:]),
            mix_conv_w[l].astype(F32), prm, _row(jnp.tile(dn_norm_g[l], DN_HEADS)),
            dn_w_proj[l].astype(BF16), pool_w[l].astype(BF16), _row(pool_scale[l]),
            _split_cols(pool_w_proj[l]), _split_cols(mix_w_out[l]),
            _row(norm_mix_g[l]), _row(norm_mix_b[l]))

        h = _ffn_ln(h.reshape(bsz * seq, D_MODEL),
                    ffn_post_w_gate[l].astype(BF16), ffn_post_w_up[l].astype(BF16),
                    _split_cols(ffn_post_w_down[l]), _row(norm_post_g[l]), _row(norm_post_b[l]),
                    "ffn_ln_post")
    return h.reshape(bsz, seq, D_MODEL)
```

```python
import numpy as np

import jax
import jax.numpy as jnp
from jax import lax
from jax.experimental import pallas as pl
from jax.experimental.pallas import tpu as pltpu

F32 = jnp.float32
BF16 = jnp.bfloat16

D_MODEL = 1024
D_FF = 2816
DN_HEADS = 4
DN_HEAD_DIM = 128
DN_WIDTH = DN_HEADS * DN_HEAD_DIM
CONV_WIDTH = 4
CHUNK = 64
POOL_WINDOWS = (2, 4, 8, 16)
POOL_GROUPS = 4
POOL_GROUP_DIM = 128
POOL_WIDTH = POOL_GROUPS * POOL_GROUP_DIM
QKV_COLS = 3 * DN_WIDTH
DEPTH = 1
ALPHA = (2.0 * DEPTH) ** 0.25
LN_EPS = 1e-5
RMS_EPS = 1e-6

SUBLANES = 8
LANES = 128
MIB = 1024 * 1024

FFN_ROWS = 512
FFN_COLS = 256
MIX_ROWS = 256
MIX_CHUNKS = MIX_ROWS // CHUNK
MIX_BATCH = 2
MIX_GROUP = 2
MIX_LAG = 6
MAX_WINDOW = max(POOL_WINDOWS)
FFN_VMEM_LIMIT = 58 * MIB
MIX_VMEM_LIMIT = 56 * MIB
SPLIT_COLS = 512


def _dot(a, b, **kw):
    return jnp.dot(a, b, preferred_element_type=F32, **kw)


def _dot_nt(a, b):
    return lax.dot_general(a, b, (((1,), (1,)), ((), ())), preferred_element_type=F32)


def _split_cols(w):
    k, n = w.shape
    return w.reshape(k, n // SPLIT_COLS, SPLIT_COLS).transpose(1, 0, 2).astype(BF16)


def _lane_split_dot(a, w_ref, rows=slice(None)):
    return jnp.concatenate([_dot(a, w_ref[i, rows, :]) for i in range(w_ref.shape[0])], axis=1)


def _layer_norm(y, g, b):
    mu = jnp.mean(y, axis=-1, keepdims=True)
    yc = y - mu
    var = jnp.mean(yc * yc, axis=-1, keepdims=True)
    return yc * lax.rsqrt(var + LN_EPS) * g + b


def _sigmoid(x):
    return 0.5 * jnp.tanh(0.5 * x) + 0.5


def _silu(x):
    half = 0.5 * x
    return half + half * jnp.tanh(half)


def _head_sums(sq, ones2):
    sq = sq.astype(BF16)
    width = ones2.shape[0]
    return jnp.concatenate(
        [_dot(sq[:, i * width:(i + 1) * width], ones2) for i in range(sq.shape[1] // width)], axis=1)


def _ffn_ln_kernel(x_ref, wg_ref, wu_ref, wd_ref, g_ref, b_ref, o_ref):
    x = x_ref[...]
    xb = x.astype(BF16)
    acc = jnp.zeros(x.shape, F32)
    for c in range(D_FF // FFN_COLS):
        cols = slice(c * FFN_COLS, (c + 1) * FFN_COLS)
        gate = _dot(xb, wg_ref[:, cols].astype(BF16))
        up = _dot(xb, wu_ref[:, cols].astype(BF16))
        acc = acc + _dot((_silu(gate) * up).astype(BF16), wd_ref[cols, :].astype(BF16))
    o_ref[...] = _layer_norm(ALPHA * x + 0.5 * acc, g_ref[...], b_ref[...])


def _resident(shape):
    return pl.BlockSpec(shape, lambda *_: (0,) * len(shape), pipeline_mode=pl.Buffered(1))


def _ffn_ln(x2d, wg, wu, wd, g, b, name):
    t = x2d.shape[0]
    return pl.pallas_call(
        _ffn_ln_kernel,
        grid=(t // FFN_ROWS,),
        in_specs=[
            pl.BlockSpec((FFN_ROWS, D_MODEL), lambda i: (i, 0)),
            _resident((D_MODEL, D_FF)),
            _resident((D_MODEL, D_FF)),
            _resident(wd.shape),
            _resident((1, D_MODEL)),
            _resident((1, D_MODEL)),
        ],
        out_specs=pl.BlockSpec((FFN_ROWS, D_MODEL), lambda i: (i, 0)),
        out_shape=jax.ShapeDtypeStruct((t, D_MODEL), F32),
        compiler_params=pltpu.CompilerParams(
            dimension_semantics=("arbitrary",), vmem_limit_bytes=FFN_VMEM_LIMIT),
        name=name,
    )(x2d, wg, wu, wd, g, b)


def _mixer_masks():
    r = np.arange(MIX_ROWS)[:, None]
    c = np.arange(MIX_ROWS)[None, :]
    block_diag = (r // CHUNK == c // CHUNK)
    in_win = np.stack([(c <= r) & (c > r - w) for w in POOL_WINDOWS])
    rt = np.arange(MAX_WINDOW)[:, None]
    ct = np.arange(MAX_WINDOW)[None, :]
    in_tail = np.stack([(ct - MAX_WINDOW > rt - w) for w in POOL_WINDOWS])
    head_pair_ones = (r // DN_HEAD_DIM == c // DN_HEAD_DIM)
    as_bf16 = lambda m: jnp.asarray(m.astype(np.float32), dtype=BF16)
    return as_bf16(block_diag), as_bf16(in_win), as_bf16(in_tail), as_bf16(head_pair_ones)


def _run_skewed(pipelines, lag):
    pending = list(pipelines)
    live = []
    step = 0
    while pending or live:
        if pending and step % lag == 0:
            live.append(pending.pop(0))
        for p in list(live):
            try:
                next(p)
            except StopIteration:
                live.remove(p)
        step += 1


def _mixer_kernel(h_ref, wqkv_ref, wz_ref, wba_ref, wp_ref, wgt_ref, convw_ref, prm_ref, dng_ref,
                  dnproj_ref, poolw_ref, pscale_ref, pproj_ref, wout_ref, lng_ref, lnb_ref,
                  bdmask_ref, inwin_ref, intail_ref, ones2_ref,
                  o_ref, state_ref, xtail_ref, ptail_ref):
    ts = MIX_ROWS
    tile = pl.program_id(1)

    @pl.when(tile == 0)
    def _():
        state_ref[...] = jnp.zeros(state_ref.shape, F32)
        xtail_ref[...] = jnp.zeros(xtail_ref.shape, F32)
        ptail_ref[...] = jnp.zeros(ptail_ref.shape, F32)

    ones2 = ones2_ref[...]
    bdmask = bdmask_ref[...]
    prow = lax.broadcasted_iota(jnp.int32, (CHUNK, ts), 0)
    plane = lax.broadcasted_iota(jnp.int32, (CHUNK, ts), 1)
    pcol = plane % CHUNK
    in_block = [plane // CHUNK == c for c in range(MIX_CHUNKS)]
    causal_p = prow >= pcol
    strict_p = prow > pcol
    eye_p = (prow == pcol).astype(F32)
    low_lanes = lax.broadcasted_iota(jnp.int32, (CHUNK, LANES), 1) < CHUNK
    t_abs = tile * ts + lax.broadcasted_iota(jnp.int32, (ts, POOL_GROUP_DIM), 0)

    def column_broadcast(row):
        return jnp.broadcast_to(row, (DN_HEAD_DIM, ts)).T

    def pack(full):
        out = full[0:CHUNK]
        for c in range(1, MIX_CHUNKS):
            out = jnp.where(in_block[c], full[c * CHUNK:(c + 1) * CHUNK], out)
        return out

    def block_diag(packed):
        return jnp.concatenate([packed.astype(BF16)] * MIX_CHUNKS, axis=0) * bdmask

    def group_stages(first):
        ng = MIX_GROUP
        seqs = range(ng)
        rows_of = lambda s: slice(s * ts, (s + 1) * ts)
        h = h_ref[first:first + ng].reshape(ng * ts, D_MODEL)
        hb = h.astype(BF16)

        ba_t = _dot_nt(wba_ref[...], hb)
        beta_t = _sigmoid(ba_t[0:SUBLANES])
        a_shift = ba_t[SUBLANES:] + prm_ref[1]
        softplus = jnp.maximum(a_shift, 0.0) + jnp.log1p(jnp.exp(-jnp.abs(a_shift)))
        gr = -jnp.exp(prm_ref[0]) * softplus
        lane_in_chunk = lax.broadcasted_iota(jnp.int32, (SUBLANES, ng * ts), 1) % CHUNK
        shift = 1
        while shift < CHUNK:
            rolled = jnp.concatenate(
                [pltpu.roll(gr[:, i * LANES:(i + 1) * LANES], shift, 1) for i in range(ng * ts // LANES)],
                axis=1)
            gr = gr + jnp.where(lane_in_chunk >= shift, rolled, 0.0)
            shift *= 2
        yield

        def gate_group(j):
            return _sigmoid(_dot(hb, wgt_ref[j]))

        qkv = []
        gate_sig = []
        for part in range(3):
            cols = slice(part * DN_WIDTH, (part + 1) * DN_WIDTH)
            raw = _dot(hb, wqkv_ref[:, cols])
            cw = convw_ref[:, cols]
            conv = []
            for s in seqs:
                raw_s = raw[rows_of(s)]
                ext = jnp.concatenate([xtail_ref[first + s, :, cols], raw_s], axis=0)
                xtail_ref[first + s, :, cols] = raw_s[ts - SUBLANES:, :]
                y = pltpu.roll(ext, CONV_WIDTH - 1, 0)[SUBLANES:] * cw[0:1]
                for tap in range(1, CONV_WIDTH - 1):
                    y = y + pltpu.roll(ext, CONV_WIDTH - 1 - tap, 0)[SUBLANES:] * cw[tap:tap + 1]
                conv.append(y + raw_s * cw[CONV_WIDTH - 1:CONV_WIDTH])
            qkv.append(_silu(jnp.concatenate(conv, axis=0)))
            gate_sig.append(gate_group(part))
            yield
        q_all = qkv[0] * (lax.rsqrt(_head_sums(qkv[0] * qkv[0], ones2) + RMS_EPS) * (DN_HEAD_DIM ** -0.5))
        k_all = qkv[1] * lax.rsqrt(_head_sums(qkv[1] * qkv[1], ones2) + RMS_EPS)
        v_all = qkv[2]

        p = _dot(hb, wp_ref[...])
        pooled = []
        for s in seqs:
            p_s = p[rows_of(s)]
            tail = ptail_ref[first + s]
            ptail_ref[first + s] = p_s[ts - MAX_WINDOW:, :]
            pooled_s = []
            for gi, win in enumerate(POOL_WINDOWS):
                lo = gi * POOL_GROUP_DIM
                pg = p_s[:, lo:lo + POOL_GROUP_DIM]
                wsum = _dot(inwin_ref[gi], pg.astype(BF16))
                top = wsum[:MAX_WINDOW] + _dot(intail_ref[gi], tail[:, lo:lo + POOL_GROUP_DIM].astype(BF16))
                wsum = jnp.concatenate([top, wsum[MAX_WINDOW:]], axis=0)
                count = jnp.minimum(t_abs + 1, win).astype(F32)
                pooled_s.append(wsum / count - pg)
            pooled.append(jnp.concatenate(pooled_s, axis=1))
        pooled = jnp.concatenate(pooled, axis=0).astype(BF16)
        pooled = jnp.concatenate(
            [_dot(pooled[:, gi * POOL_GROUP_DIM:(gi + 1) * POOL_GROUP_DIM], poolw_ref[gi])
             for gi in range(POOL_GROUPS)], axis=1) * pscale_ref[...]
        y_pool = _lane_split_dot(pooled.astype(BF16), pproj_ref)
        z = _dot(hb, wz_ref[...])
        gate_sig.append(gate_group(3))
        yield

        chains = [(s, hd) for s in seqs for hd in range(DN_HEADS)]
        ids = range(len(chains))
        qn, kn_b, k_beta, v_beta, eg_b, kd_t, decay_p = [], [], [], [], [], [], []
        for s, hd in chains:
            lanes = slice(hd * DN_HEAD_DIM, (hd + 1) * DN_HEAD_DIM)
            qn.append(q_all[rows_of(s), lanes])
            kn = k_all[rows_of(s), lanes]
            v = v_all[rows_of(s), lanes]
            beta_b = column_broadcast(beta_t[hd:hd + 1, rows_of(s)])
            gr_row = gr[hd:hd + 1, rows_of(s)]
            gc_b = column_broadcast(gr_row)
            gc_p = jnp.concatenate(
                [jnp.where(low_lanes, gc_b[2 * i * CHUNK:(2 * i + 1) * CHUNK],
                           gc_b[(2 * i + 1) * CHUNK:(2 * i + 2) * CHUNK])
                 for i in range(MIX_CHUNKS // 2)], axis=1)
            decay_p.append(jnp.exp(jnp.where(causal_p, gc_p - gr_row, -jnp.inf)))
            eg_b.append(jnp.exp(gc_b))
            gl_b = jnp.concatenate(
                [jnp.broadcast_to(gc_b[c * CHUNK + CHUNK - 1:(c + 1) * CHUNK, :], (CHUNK, DN_HEAD_DIM))
                 for c in range(MIX_CHUNKS)], axis=0)
            k_beta.append(kn * beta_b)
            v_beta.append(v * beta_b)
            kn_b.append(kn.astype(BF16))
            kd_t.append((kn * jnp.exp(gl_b - gc_b)).T.astype(BF16))
        gram = [_dot_nt(jnp.concatenate([k_beta[i], qn[i]], axis=0).astype(BF16), kn_b[i]) for i in ids]
        yield
        lower = [pack(gram[i][:ts]) * jnp.where(strict_p, decay_p[i], 0.0) for i in ids]
        qk = [block_diag(pack(gram[i][ts:]) * decay_p[i]) for i in ids]
        t_mat = [eye_p - lower[i] for i in ids]
        power = [_dot(lower[i].astype(BF16), block_diag(lower[i])) for i in ids]
        yield
        for _ in range(4):
            both = [_dot(jnp.concatenate([power[i], t_mat[i]], axis=0).astype(BF16), block_diag(power[i]))
                    for i in ids]
            power = [both[i][:CHUNK] for i in ids]
            t_mat = [t_mat[i] + both[i][CHUNK:] for i in ids]
            yield
        t_mat = [t_mat[i] + _dot(t_mat[i].astype(BF16), block_diag(power[i])) for i in ids]
        yield
        uw = [_dot(block_diag(t_mat[i]),
                   jnp.concatenate([v_beta[i], k_beta[i] * eg_b[i]], axis=1).astype(BF16)) for i in ids]
        q_dec = [qn[i] * eg_b[i] for i in ids]
        state = [state_ref[first + s, hd] for s, hd in chains]
        yield
        kd_uw = []
        for i in ids:
            per_chunk = []
            for c in range(MIX_CHUNKS):
                uw_pad = jnp.concatenate(
                    [uw[i][cc * CHUNK:(cc + 1) * CHUNK] if cc == c else jnp.zeros((CHUNK, 2 * DN_HEAD_DIM), F32)
                     for cc in range(MIX_CHUNKS)], axis=0).astype(BF16)
                per_chunk.append(_dot(kd_t[i], uw_pad))
            kd_uw.append(per_chunk)
        yield
        o_inter = [[] for _ in ids]
        v_new_all = [[] for _ in ids]
        for c in range(MIX_CHUNKS):
            r0 = c * CHUNK
            for i in ids:
                u_c = uw[i][r0:r0 + CHUNK, :DN_HEAD_DIM]
                w_c = uw[i][r0:r0 + CHUNK, DN_HEAD_DIM:]
                lhs = jnp.concatenate(
                    [kd_uw[i][c][:, DN_HEAD_DIM:], w_c, q_dec[i][r0:r0 + CHUNK]], axis=0).astype(BF16)
                prod = _dot(lhs, state[i].astype(BF16))
                v_new_all[i].append(u_c - prod[DN_HEAD_DIM:DN_HEAD_DIM + CHUNK])
                o_inter[i].append(prod[DN_HEAD_DIM + CHUNK:])
                g_last = eg_b[i][r0 + CHUNK - 1:r0 + CHUNK, :]
                state[i] = state[i] * g_last - prod[:DN_HEAD_DIM] + kd_uw[i][c][:, :DN_HEAD_DIM]
            yield
        o_chain = []
        for i, (s, hd) in enumerate(chains):
            state_ref[first + s, hd] = state[i]
            v_new_full = jnp.concatenate(v_new_all[i], axis=0).astype(BF16)
            o_chain.append(jnp.concatenate(o_inter[i], axis=0) + _dot(qk[i], v_new_full))
        o_all = jnp.concatenate(
            [jnp.concatenate(o_chain[s * DN_HEADS:(s + 1) * DN_HEADS], axis=1) for s in seqs], axis=0)
        o_all = o_all * lax.rsqrt(_head_sums(o_all * o_all, ones2) * (1.0 / DN_HEAD_DIM) + RMS_EPS)
        o_all = o_all * dng_ref[...] * _silu(z)
        y_dn = _dot(o_all.astype(BF16), dnproj_ref[...])
        yield

        merged = (jnp.concatenate(gate_sig[:2], axis=1) * y_dn
                  + jnp.concatenate(gate_sig[2:], axis=1) * y_pool)
        m = _lane_split_dot(merged.astype(BF16), wout_ref)
        out = _layer_norm(ALPHA * h + m, lng_ref[...], lnb_ref[...])
        o_ref[first:first + ng] = out.reshape(ng, ts, D_MODEL)
        yield

    _run_skewed([group_stages(first) for first in range(0, MIX_BATCH, MIX_GROUP)], MIX_LAG)


def _mixer_ln(h3d, wqkv, wz, wba, wp, wgt, convw, prm, dng, dnproj, poolw, pscale, pproj, wout, lng, lnb):
    b, s, _ = h3d.shape
    operands = (wqkv, wz, wba, wp, wgt, convw, prm, dng, dnproj, poolw, pscale, pproj, wout, lng, lnb,
                *_mixer_masks())
    block = (MIX_BATCH, MIX_ROWS, D_MODEL)
    return pl.pallas_call(
        _mixer_kernel,
        grid=(b // MIX_BATCH, s // MIX_ROWS),
        in_specs=[pl.BlockSpec(block, lambda i, j: (i, j, 0))] + [_resident(op.shape) for op in operands],
        out_specs=pl.BlockSpec(block, lambda i, j: (i, j, 0)),
        out_shape=jax.ShapeDtypeStruct(h3d.shape, F32),
        scratch_shapes=[
            pltpu.VMEM((MIX_BATCH, DN_HEADS, DN_HEAD_DIM, DN_HEAD_DIM), F32),
            pltpu.VMEM((MIX_BATCH, SUBLANES, QKV_COLS), F32),
            pltpu.VMEM((MIX_BATCH, MAX_WINDOW, POOL_WIDTH), F32),
        ],
        compiler_params=pltpu.CompilerParams(
            dimension_semantics=("arbitrary", "arbitrary"), vmem_limit_bytes=MIX_VMEM_LIMIT),
        name="mixer_ln",
    )(h3d, *operands)


def _row(v):
    return v.reshape(1, -1).astype(F32)


def kernel(x, ffn_pre_w_gate, ffn_pre_w_up, ffn_pre_w_down, norm_pre_g, norm_pre_b, mix_w_in, mix_conv_w, dn_a_log, dn_dt_bias, dn_norm_g, dn_w_proj, pool_w, pool_scale, pool_w_proj, mix_w_out, norm_mix_g, norm_mix_b, ffn_post_w_gate, ffn_post_w_up, ffn_post_w_down, norm_post_g, norm_post_b):
    bsz, seq, _ = x.shape
    h = x
    for l in range(DEPTH):
        h = _ffn_ln(h.reshape(bsz * seq, D_MODEL),
                    ffn_pre_w_gate[l], ffn_pre_w_up[l], ffn_pre_w_down[l], _row(norm_pre_g[l]), _row(norm_pre_b[l]),
                    "ffn_ln_pre")

        w_in = mix_w_in[l]
        c0 = QKV_COLS
        c1 = c0 + DN_WIDTH
        c2 = c1 + DN_HEADS
        c3 = c2 + DN_HEADS
        c4 = c3 + POOL_WIDTH
        wba_t = jnp.zeros((2 * SUBLANES, D_MODEL), F32)
        wba_t = wba_t.at[:DN_HEADS].set(w_in[:, c1:c2].T).at[SUBLANES:SUBLANES + DN_HEADS].set(w_in[:, c2:c3].T)
        prm = jnp.zeros((2, SUBLANES, MIX_GROUP * MIX_ROWS), F32)
        prm = prm.at[0, :DN_HEADS].set(jnp.broadcast_to(dn_a_log[l][:, None], (DN_HEADS, MIX_GROUP * MIX_ROWS)))
        prm = prm.at[1, :DN_HEADS].set(jnp.broadcast_to(dn_dt_bias[l][:, None], (DN_HEADS, MIX_GROUP * MIX_ROWS)))
        h = _mixer_ln(
            h.reshape(bsz, seq, D_MODEL),
            w_in[:, :c0].astype(BF16), w_in[:, c0:c1].astype(BF16), wba_t.astype(BF16),
            w_in[:, c3:c4].astype(BF16), _split_cols(w_in[:, c4:]),
            mix_conv_w[l].astype(F32), prm, _row(jnp.tile(dn_norm_g[l], DN_HEADS)),
            dn_w_proj[l].astype(BF16), pool_w[l].astype(BF16), _row(pool_scale[l]),
            _split_cols(pool_w_proj[l]), _split_cols(mix_w_out[l]),
            _row(norm_mix_g[l]), _row(norm_mix_b[l]))

        h = _ffn_ln(h.reshape(bsz * seq, D_MODEL),
                    ffn_post_w_gate[l], ffn_post_w_up[l], ffn_post_w_down[l], _row(norm_post_g[l]), _row(norm_post_b[l]),
                    "ffn_ln_post")
    return h.reshape(bsz, seq, D_MODEL)
```

```python
import numpy as np

import jax
import jax.numpy as jnp
from jax import lax
from jax.experimental import pallas as pl
from jax.experimental.pallas import tpu as pltpu

F32 = jnp.float32
BF16 = jnp.bfloat16

D_MODEL = 1024
D_FF = 2816
DN_HEADS = 4
DN_HEAD_DIM = 128
DN_WIDTH = DN_HEADS * DN_HEAD_DIM
CONV_WIDTH = 4
CHUNK = 64
POOL_WINDOWS = (2, 4, 8, 16)
POOL_GROUPS = 4
POOL_GROUP_DIM = 128
POOL_WIDTH = POOL_GROUPS * POOL_GROUP_DIM
QKV_COLS = 3 * DN_WIDTH
DEPTH = 1
ALPHA = (2.0 * DEPTH) ** 0.25
LN_EPS = 1e-5
RMS_EPS = 1e-6

SUBLANES = 8
LANES = 128
MIB = 1024 * 1024

FFN_ROWS = 512
FFN_EPILOGUE_ROWS = 64
FFN_COLS = 256
MIX_ROWS = 256
MIX_CHUNKS = MIX_ROWS // CHUNK
MIX_BATCH = 2
MIX_GROUP = 2
MIX_LAG = 6
MAX_WINDOW = max(POOL_WINDOWS)
FFN_VMEM_LIMIT = 58 * MIB
MIX_VMEM_LIMIT = 56 * MIB
SPLIT_COLS = 512


def _dot(a, b, **kw):
    return jnp.dot(a, b, preferred_element_type=F32, **kw)


def _dot_nt(a, b):
    return lax.dot_general(a, b, (((1,), (1,)), ((), ())), preferred_element_type=F32)


def _split_cols(w):
    n = w.shape[1]
    return jnp.stack([w[:, i:i + SPLIT_COLS].astype(BF16) for i in range(0, n, SPLIT_COLS)])


def _lane_split_dot(a, w_ref, rows=slice(None)):
    return jnp.concatenate([_dot(a, w_ref[i, rows, :]) for i in range(w_ref.shape[0])], axis=1)


def _layer_norm(y, g, b):
    mu = jnp.mean(y, axis=-1, keepdims=True)
    yc = y - mu
    var = jnp.mean(yc * yc, axis=-1, keepdims=True)
    return yc * lax.rsqrt(var + LN_EPS) * g + b


def _sigmoid(x):
    return 0.5 * jnp.tanh(0.5 * x) + 0.5


def _silu(x):
    half = 0.5 * x
    return half + half * jnp.tanh(half)


def _head_sums(sq, ones2):
    sq = sq.astype(BF16)
    width = ones2.shape[0]
    return jnp.concatenate(
        [_dot(sq[:, i * width:(i + 1) * width], ones2) for i in range(sq.shape[1] // width)], axis=1)


def _issued_after(value, anchor):
    never = (anchor == anchor) & (anchor != anchor)
    return value + jnp.where(never, 1.0, 0.0)


def _ffn_ln_kernel(x_ref, x_prev_ref, wg_ref, wu_ref, wd_ref, g_ref, b_ref, o_ref, acc_keep):
    i = pl.program_id(0)
    last = pl.num_programs(0) - 1
    pieces = FFN_ROWS // FFN_EPILOGUE_ROWS
    n_chunks = D_FF // FFN_COLS
    assert pieces + 2 <= n_chunks

    def epilogue_piece(k, anchor=None):
        rows = slice(k * FFN_EPILOGUE_ROWS, (k + 1) * FFN_EPILOGUE_ROWS)
        x_rows = x_prev_ref[rows, :]
        if anchor is not None:
            x_rows = _issued_after(x_rows, anchor)
        y = ALPHA * x_rows + 0.5 * acc_keep[rows, :]
        o_ref[rows, :] = _layer_norm(y, g_ref[...], b_ref[...])

    @pl.when(i == 0)
    def _():
        acc_keep[...] = jnp.zeros(acc_keep.shape, F32)

    @pl.when(i < last)
    def _():
        x = x_ref[...]
        xb = x.astype(BF16)
        acc = jnp.zeros(x.shape, F32)
        for c in range(n_chunks):
            cols = slice(c * FFN_COLS, (c + 1) * FFN_COLS)
            gate = _dot(xb, wg_ref[:, cols].astype(BF16))
            up = _dot(xb, wu_ref[:, cols].astype(BF16))
            acc = acc + _dot((_silu(gate) * up).astype(BF16), wd_ref[cols, :].astype(BF16))
            if 2 <= c < pieces + 2:
                epilogue_piece(c - 2, acc[0:1, :])
        acc_keep[...] = acc

    @pl.when(i == last)
    def _():
        for k in range(pieces):
            epilogue_piece(k)


def _resident(shape):
    return pl.BlockSpec(shape, lambda *_: (0,) * len(shape), pipeline_mode=pl.Buffered(1))


def _ffn_ln(x2d, wg, wu, wd, g, b, name):
    t = x2d.shape[0]
    tiles = t // FFN_ROWS
    return pl.pallas_call(
        _ffn_ln_kernel,
        grid=(tiles + 1,),
        in_specs=[
            pl.BlockSpec((FFN_ROWS, D_MODEL), lambda i: (jnp.minimum(i, tiles - 1), 0)),
            pl.BlockSpec((FFN_ROWS, D_MODEL), lambda i: (jnp.maximum(i - 1, 0), 0)),
            _resident((D_MODEL, D_FF)),
            _resident((D_MODEL, D_FF)),
            _resident(wd.shape),
            _resident((1, D_MODEL)),
            _resident((1, D_MODEL)),
        ],
        out_specs=pl.BlockSpec((FFN_ROWS, D_MODEL), lambda i: (jnp.maximum(i - 1, 0), 0)),
        out_shape=jax.ShapeDtypeStruct((t, D_MODEL), F32),
        scratch_shapes=[
            pltpu.VMEM((FFN_ROWS, D_MODEL), F32),
        ],
        compiler_params=pltpu.CompilerParams(
            dimension_semantics=("arbitrary",), vmem_limit_bytes=FFN_VMEM_LIMIT),
        name=name,
    )(x2d, x2d, wg, wu, wd, g, b)


def _mixer_masks():
    r = np.arange(MIX_ROWS)[:, None]
    c = np.arange(MIX_ROWS)[None, :]
    block_diag = (r // CHUNK == c // CHUNK)
    in_win = np.stack([(c <= r) & (c > r - w) for w in POOL_WINDOWS])
    rt = np.arange(MAX_WINDOW)[:, None]
    ct = np.arange(MAX_WINDOW)[None, :]
    in_tail = np.stack([(ct - MAX_WINDOW > rt - w) for w in POOL_WINDOWS])
    head_pair_ones = (r // DN_HEAD_DIM == c // DN_HEAD_DIM)
    as_bf16 = lambda m: jnp.asarray(m.astype(np.float32), dtype=BF16)
    return as_bf16(block_diag), as_bf16(in_win), as_bf16(in_tail), as_bf16(head_pair_ones)


def _run_skewed(pipelines, lag):
    pending = list(pipelines)
    live = []
    step = 0
    while pending or live:
        if pending and step % lag == 0:
            live.append(pending.pop(0))
        for p in list(live):
            try:
                next(p)
            except StopIteration:
                live.remove(p)
        step += 1


def _mixer_kernel(h_ref, wqkv_ref, wz_ref, wba_ref, wp_ref, wgt_ref, convw_ref, prm_ref, dng_ref,
                  dnproj_ref, poolw_ref, pscale_ref, pproj_ref, wout_ref, lng_ref, lnb_ref,
                  bdmask_ref, inwin_ref, intail_ref, ones2_ref,
                  o_ref, state_ref, xtail_ref, ptail_ref):
    ts = MIX_ROWS
    tile = pl.program_id(1)

    @pl.when(tile == 0)
    def _():
        state_ref[...] = jnp.zeros(state_ref.shape, F32)
        xtail_ref[...] = jnp.zeros(xtail_ref.shape, F32)
        ptail_ref[...] = jnp.zeros(ptail_ref.shape, F32)

    ones2 = ones2_ref[...]
    bdmask = bdmask_ref[...]
    prow = lax.broadcasted_iota(jnp.int32, (CHUNK, ts), 0)
    plane = lax.broadcasted_iota(jnp.int32, (CHUNK, ts), 1)
    pcol = plane % CHUNK
    in_block = [plane // CHUNK == c for c in range(MIX_CHUNKS)]
    causal_p = prow >= pcol
    strict_p = prow > pcol
    eye_p = (prow == pcol).astype(F32)
    low_lanes = lax.broadcasted_iota(jnp.int32, (CHUNK, LANES), 1) < CHUNK
    t_abs = tile * ts + lax.broadcasted_iota(jnp.int32, (ts, POOL_GROUP_DIM), 0)

    def column_broadcast(row):
        return jnp.broadcast_to(row, (DN_HEAD_DIM, ts)).T

    def pack(full):
        out = full[0:CHUNK]
        for c in range(1, MIX_CHUNKS):
            out = jnp.where(in_block[c], full[c * CHUNK:(c + 1) * CHUNK], out)
        return out

    def block_diag(packed):
        return jnp.concatenate([packed.astype(BF16)] * MIX_CHUNKS, axis=0) * bdmask

    def group_stages(first):
        ng = MIX_GROUP
        seqs = range(ng)
        rows_of = lambda s: slice(s * ts, (s + 1) * ts)
        h = h_ref[first:first + ng].reshape(ng * ts, D_MODEL)
        hb = h.astype(BF16)

        def write_strength_and_decay():
            ba_t = _dot_nt(wba_ref[...], hb)
            beta = _sigmoid(ba_t[0:SUBLANES])
            a_shift = ba_t[SUBLANES:] + prm_ref[1]
            softplus = jnp.maximum(a_shift, 0.0) + jnp.log1p(jnp.exp(-jnp.abs(a_shift)))
            g = -jnp.exp(prm_ref[0]) * softplus
            lane_in_chunk = lax.broadcasted_iota(jnp.int32, (SUBLANES, ng * ts), 1) % CHUNK
            shift = 1
            while shift < CHUNK:
                rolled = jnp.concatenate(
                    [pltpu.roll(g[:, i * LANES:(i + 1) * LANES], shift, 1) for i in range(ng * ts // LANES)],
                    axis=1)
                g = g + jnp.where(lane_in_chunk >= shift, rolled, 0.0)
                shift *= 2
            return beta, g

        def gate_group(j):
            return _sigmoid(_dot(hb, wgt_ref[j]))

        qkv = []
        gate_sig = []
        for part in range(3):
            cols = slice(part * DN_WIDTH, (part + 1) * DN_WIDTH)
            raw = _dot(hb, wqkv_ref[:, cols])
            cw = convw_ref[:, cols]
            conv = []
            for s in seqs:
                raw_s = raw[rows_of(s)]
                ext = jnp.concatenate([xtail_ref[first + s, :, cols], raw_s], axis=0)
                xtail_ref[first + s, :, cols] = raw_s[ts - SUBLANES:, :]
                y = pltpu.roll(ext, CONV_WIDTH - 1, 0)[SUBLANES:] * cw[0:1]
                for tap in range(1, CONV_WIDTH - 1):
                    y = y + pltpu.roll(ext, CONV_WIDTH - 1 - tap, 0)[SUBLANES:] * cw[tap:tap + 1]
                conv.append(y + raw_s * cw[CONV_WIDTH - 1:CONV_WIDTH])
            qkv.append(_silu(jnp.concatenate(conv, axis=0)))
            gate_sig.append(gate_group(part))
            if part == 1:
                beta_t, gr = write_strength_and_decay()
            yield
        q_all = qkv[0] * (lax.rsqrt(_head_sums(qkv[0] * qkv[0], ones2) + RMS_EPS) * (DN_HEAD_DIM ** -0.5))
        k_all = qkv[1] * lax.rsqrt(_head_sums(qkv[1] * qkv[1], ones2) + RMS_EPS)
        v_all = qkv[2]

        p = _dot(hb, wp_ref[...])
        pooled = []
        for s in seqs:
            p_s = p[rows_of(s)]
            tail = ptail_ref[first + s]
            ptail_ref[first + s] = p_s[ts - MAX_WINDOW:, :]
            pooled_s = []
            for gi, win in enumerate(POOL_WINDOWS):
                lo = gi * POOL_GROUP_DIM
                pg = p_s[:, lo:lo + POOL_GROUP_DIM]
                wsum = _dot(inwin_ref[gi], pg.astype(BF16))
                top = wsum[:MAX_WINDOW] + _dot(intail_ref[gi], tail[:, lo:lo + POOL_GROUP_DIM].astype(BF16))
                wsum = jnp.concatenate([top, wsum[MAX_WINDOW:]], axis=0)
                count = jnp.minimum(t_abs + 1, win).astype(F32)
                pooled_s.append(wsum / count - pg)
            pooled.append(jnp.concatenate(pooled_s, axis=1))
        pooled = jnp.concatenate(pooled, axis=0).astype(BF16)
        pooled = jnp.concatenate(
            [_dot(pooled[:, gi * POOL_GROUP_DIM:(gi + 1) * POOL_GROUP_DIM], poolw_ref[gi])
             for gi in range(POOL_GROUPS)], axis=1) * pscale_ref[...]
        y_pool = _lane_split_dot(pooled.astype(BF16), pproj_ref)
        z = _dot(hb, wz_ref[...])
        gate_sig.append(gate_group(3))
        yield

        chains = [(s, hd) for s in seqs for hd in range(DN_HEADS)]
        ids = range(len(chains))
        qn, kn_b, k_beta, v_beta, eg_b, kd_t, decay_p = [], [], [], [], [], [], []
        for s, hd in chains:
            lanes = slice(hd * DN_HEAD_DIM, (hd + 1) * DN_HEAD_DIM)
            qn.append(q_all[rows_of(s), lanes])
            kn = k_all[rows_of(s), lanes]
            v = v_all[rows_of(s), lanes]
            beta_b = column_broadcast(beta_t[hd:hd + 1, rows_of(s)])
            gr_row = gr[hd:hd + 1, rows_of(s)]
            gc_b = column_broadcast(gr_row)
            gc_p = jnp.concatenate(
                [jnp.where(low_lanes, gc_b[2 * i * CHUNK:(2 * i + 1) * CHUNK],
                           gc_b[(2 * i + 1) * CHUNK:(2 * i + 2) * CHUNK])
                 for i in range(MIX_CHUNKS // 2)], axis=1)
            decay_p.append(jnp.exp(jnp.where(causal_p, gc_p - gr_row, -jnp.inf)))
            eg_b.append(jnp.exp(gc_b))
            gl_b = jnp.concatenate(
                [jnp.broadcast_to(gc_b[c * CHUNK + CHUNK - 1:(c + 1) * CHUNK, :], (CHUNK, DN_HEAD_DIM))
                 for c in range(MIX_CHUNKS)], axis=0)
            k_beta.append(kn * beta_b)
            v_beta.append(v * beta_b)
            kn_b.append(kn.astype(BF16))
            kd_t.append((kn * jnp.exp(gl_b - gc_b)).T.astype(BF16))
        gram = [_dot_nt(jnp.concatenate([k_beta[i], qn[i]], axis=0).astype(BF16), kn_b[i]) for i in ids]
        yield
        lower = [pack(gram[i][:ts]) * jnp.where(strict_p, decay_p[i], 0.0) for i in ids]
        qk = [block_diag(pack(gram[i][ts:]) * decay_p[i]) for i in ids]
        t_mat = [eye_p - lower[i] for i in ids]
        power = [_dot(lower[i].astype(BF16), block_diag(lower[i])) for i in ids]
        yield
        for _ in range(4):
            both = [_dot(jnp.concatenate([power[i], t_mat[i]], axis=0).astype(BF16), block_diag(power[i]))
                    for i in ids]
            power = [both[i][:CHUNK] for i in ids]
            t_mat = [t_mat[i] + both[i][CHUNK:] for i in ids]
            yield
        t_mat = [t_mat[i] + _dot(t_mat[i].astype(BF16), block_diag(power[i])) for i in ids]
        yield
        uw = [_dot(block_diag(t_mat[i]),
                   jnp.concatenate([v_beta[i], k_beta[i] * eg_b[i]], axis=1).astype(BF16)) for i in ids]
        q_dec = [qn[i] * eg_b[i] for i in ids]
        state = [state_ref[first + s, hd] for s, hd in chains]
        yield
        kd_uw = []
        for i in ids:
            per_chunk = []
            for c in range(MIX_CHUNKS):
                uw_pad = jnp.concatenate(
                    [uw[i][cc * CHUNK:(cc + 1) * CHUNK] if cc == c else jnp.zeros((CHUNK, 2 * DN_HEAD_DIM), F32)
                     for cc in range(MIX_CHUNKS)], axis=0).astype(BF16)
                per_chunk.append(_dot(kd_t[i], uw_pad))
            kd_uw.append(per_chunk)
        yield
        o_inter = [[] for _ in ids]
        v_new_all = [[] for _ in ids]
        for c in range(MIX_CHUNKS):
            r0 = c * CHUNK
            for i in ids:
                u_c = uw[i][r0:r0 + CHUNK, :DN_HEAD_DIM]
                w_c = uw[i][r0:r0 + CHUNK, DN_HEAD_DIM:]
                lhs = jnp.concatenate(
                    [kd_uw[i][c][:, DN_HEAD_DIM:], w_c, q_dec[i][r0:r0 + CHUNK]], axis=0).astype(BF16)
                prod = _dot(lhs, state[i].astype(BF16))
                v_new_all[i].append(u_c - prod[DN_HEAD_DIM:DN_HEAD_DIM + CHUNK])
                o_inter[i].append(prod[DN_HEAD_DIM + CHUNK:])
                g_last = eg_b[i][r0 + CHUNK - 1:r0 + CHUNK, :]
                state[i] = state[i] * g_last - prod[:DN_HEAD_DIM] + kd_uw[i][c][:, :DN_HEAD_DIM]
            yield
        o_chain = []
        for i, (s, hd) in enumerate(chains):
            state_ref[first + s, hd] = state[i]
            v_new_full = jnp.concatenate(v_new_all[i], axis=0).astype(BF16)
            o_chain.append(jnp.concatenate(o_inter[i], axis=0) + _dot(qk[i], v_new_full))
        o_all = jnp.concatenate(
            [jnp.concatenate(o_chain[s * DN_HEADS:(s + 1) * DN_HEADS], axis=1) for s in seqs], axis=0)
        o_all = o_all * lax.rsqrt(_head_sums(o_all * o_all, ones2) * (1.0 / DN_HEAD_DIM) + RMS_EPS)
        o_all = o_all * dng_ref[...] * _silu(z)
        y_dn = _dot(o_all.astype(BF16), dnproj_ref[...])
        yield

        merged = (jnp.concatenate(gate_sig[:2], axis=1) * y_dn
                  + jnp.concatenate(gate_sig[2:], axis=1) * y_pool)
        m = _lane_split_dot(merged.astype(BF16), wout_ref)
        out = _layer_norm(ALPHA * h + m, lng_ref[...], lnb_ref[...])
        o_ref[first:first + ng] = out.reshape(ng, ts, D_MODEL)
        yield

    _run_skewed([group_stages(first) for first in range(0, MIX_BATCH, MIX_GROUP)], MIX_LAG)


def _mixer_ln(h3d, wqkv, wz, wba, wp, wgt, convw, prm, dng, dnproj, poolw, pscale, pproj, wout, lng, lnb):
    b, s, _ = h3d.shape
    operands = (wqkv, wz, wba, wp, wgt, convw, prm, dng, dnproj, poolw, pscale, pproj, wout, lng, lnb,
                *_mixer_masks())
    block = (MIX_BATCH, MIX_ROWS, D_MODEL)
    return pl.pallas_call(
        _mixer_kernel,
        grid=(b // MIX_BATCH, s // MIX_ROWS),
        in_specs=[pl.BlockSpec(block, lambda i, j: (i, j, 0))] + [_resident(op.shape) for op in operands],
        out_specs=pl.BlockSpec(block, lambda i, j: (i, j, 0)),
        out_shape=jax.ShapeDtypeStruct(h3d.shape, F32),
        scratch_shapes=[
            pltpu.VMEM((MIX_BATCH, DN_HEADS, DN_HEAD_DIM, DN_HEAD_DIM), F32),
            pltpu.VMEM((MIX_BATCH, SUBLANES, QKV_COLS), F32),
            pltpu.VMEM((MIX_BATCH, MAX_WINDOW, POOL_WIDTH), F32),
        ],
        compiler_params=pltpu.CompilerParams(
            dimension_semantics=("arbitrary", "arbitrary"), vmem_limit_bytes=MIX_VMEM_LIMIT),
        name="mixer_ln",
    )(h3d, *operands)


def _row(v):
    return v.reshape(1, -1).astype(F32)


def kernel(x, ffn_pre_w_gate, ffn_pre_w_up, ffn_pre_w_down, norm_pre_g, norm_pre_b, mix_w_in, mix_conv_w, dn_a_log, dn_dt_bias, dn_norm_g, dn_w_proj, pool_w, pool_scale, pool_w_proj, mix_w_out, norm_mix_g, norm_mix_b, ffn_post_w_gate, ffn_post_w_up, ffn_post_w_down, norm_post_g, norm_post_b):
    bsz, seq, _ = x.shape
    h = x
    for l in range(DEPTH):
        h = _ffn_ln(h.reshape(bsz * seq, D_MODEL),
                    ffn_pre_w_gate[l], ffn_pre_w_up[l], ffn_pre_w_down[l], _row(norm_pre_g[l]), _row(norm_pre_b[l]),
                    "ffn_ln_pre")

        w_in = mix_w_in[l]
        c0 = QKV_COLS
        c1 = c0 + DN_WIDTH
        c2 = c1 + DN_HEADS
        c3 = c2 + DN_HEADS
        c4 = c3 + POOL_WIDTH
        wba_t = jnp.zeros((2 * SUBLANES, D_MODEL), F32)
        wba_t = wba_t.at[:DN_HEADS].set(w_in[:, c1:c2].T).at[SUBLANES:SUBLANES + DN_HEADS].set(w_in[:, c2:c3].T)
        prm = jnp.zeros((2, SUBLANES, MIX_GROUP * MIX_ROWS), F32)
        prm = prm.at[0, :DN_HEADS].set(jnp.broadcast_to(dn_a_log[l][:, None], (DN_HEADS, MIX_GROUP * MIX_ROWS)))
        prm = prm.at[1, :DN_HEADS].set(jnp.broadcast_to(dn_dt_bias[l][:, None], (DN_HEADS, MIX_GROUP * MIX_ROWS)))
        h = _mixer_ln(
            h.reshape(bsz, seq, D_MODEL),
            w_in[:, :c0].astype(BF16), w_in[:, c0:c1].astype(BF16), wba_t.astype(BF16),
            w_in[:, c3:c4].astype(BF16), _split_cols(w_in[:, c4:]),
            mix_conv_w[l].astype(F32), prm, _row(jnp.tile(dn_norm_g[l], DN_HEADS)),
            dn_w_proj[l].astype(BF16), pool_w[l].astype(BF16), _row(pool_scale[l]),
            _split_cols(pool_w_proj[l]), _split_cols(mix_w_out[l]),
            _row(norm_mix_g[l]), _row(norm_mix_b[l]))

        h = _ffn_ln(h.reshape(bsz * seq, D_MODEL),
                    ffn_post_w_gate[l], ffn_post_w_up[l], ffn_post_w_down[l], _row(norm_post_g[l]), _row(norm_post_b[l]),
                    "ffn_ln_post")
    return h.reshape(bsz, seq, D_MODEL)
```

```python
import numpy as np

import jax
import jax.numpy as jnp
from jax import lax
from jax.experimental import pallas as pl
from jax.experimental.pallas import tpu as pltpu

F32 = jnp.float32
BF16 = jnp.bfloat16

D_MODEL = 1024
D_FF = 2816
DN_HEADS = 4
DN_HEAD_DIM = 128
DN_WIDTH = DN_HEADS * DN_HEAD_DIM
CONV_WIDTH = 4
CHUNK = 64
POOL_WINDOWS = (2, 4, 8, 16)
POOL_GROUPS = 4
POOL_GROUP_DIM = 128
POOL_WIDTH = POOL_GROUPS * POOL_GROUP_DIM
QKV_COLS = 3 * DN_WIDTH
W_Z0 = QKV_COLS
W_BA0 = W_Z0 + DN_WIDTH
W_P0 = W_BA0 + 2 * DN_HEADS
W_GATE0 = W_P0 + POOL_WIDTH
DEPTH = 1
ALPHA = (2.0 * DEPTH) ** 0.25
LN_EPS = 1e-5
RMS_EPS = 1e-6

SUBLANES = 8
LANES = 128
MIB = 1024 * 1024

FFN_ROWS = 512
FFN_EPILOGUE_ROWS = 64
FFN_COLS = 256
MIX_ROWS = 256
MIX_CHUNKS = MIX_ROWS // CHUNK
MIX_BATCH = 2
MIX_GROUP = 2
MIX_LAG = 6
MAX_WINDOW = max(POOL_WINDOWS)
FFN_VMEM_LIMIT = 58 * MIB
MIX_VMEM_LIMIT = 56 * MIB


def _dot(a, b, **kw):
    return jnp.dot(a, b, preferred_element_type=F32, **kw)


def _dot_nt(a, b):
    return lax.dot_general(a, b, (((1,), (1,)), ((), ())), preferred_element_type=F32)


def _layer_norm(y, g, b):
    mu = jnp.mean(y, axis=-1, keepdims=True)
    yc = y - mu
    var = jnp.mean(yc * yc, axis=-1, keepdims=True)
    return yc * lax.rsqrt(var + LN_EPS) * g + b


def _sigmoid(x):
    return 0.5 * jnp.tanh(0.5 * x) + 0.5


def _silu(x):
    half = 0.5 * x
    return half + half * jnp.tanh(half)


def _head_sums(sq, ones2):
    sq = sq.astype(BF16)
    width = ones2.shape[0]
    return jnp.concatenate(
        [_dot(sq[:, i * width:(i + 1) * width], ones2) for i in range(sq.shape[1] // width)], axis=1)


def _issued_after(value, anchor):
    never = (anchor == anchor) & (anchor != anchor)
    return value + jnp.where(never, 1.0, 0.0)


def _ffn_ln_kernel(x_ref, x_prev_ref, wg_ref, wu_ref, wd_ref, g_ref, b_ref, o_ref, acc_keep):
    i = pl.program_id(0)
    last = pl.num_programs(0) - 1
    pieces = FFN_ROWS // FFN_EPILOGUE_ROWS
    n_chunks = D_FF // FFN_COLS
    assert pieces + 2 <= n_chunks

    def epilogue_piece(k, anchor=None):
        rows = slice(k * FFN_EPILOGUE_ROWS, (k + 1) * FFN_EPILOGUE_ROWS)
        x_rows = x_prev_ref[rows, :]
        if anchor is not None:
            x_rows = _issued_after(x_rows, anchor)
        y = ALPHA * x_rows + 0.5 * acc_keep[rows, :]
        o_ref[rows, :] = _layer_norm(y, g_ref[...], b_ref[...])

    @pl.when(i == 0)
    def _():
        acc_keep[...] = jnp.zeros(acc_keep.shape, F32)

    @pl.when(i < last)
    def _():
        x = x_ref[...]
        xb = x.astype(BF16)
        acc = jnp.zeros(x.shape, F32)
        for c in range(n_chunks):
            cols = slice(c * FFN_COLS, (c + 1) * FFN_COLS)
            gate = _dot(xb, wg_ref[:, cols].astype(BF16))
            up = _dot(xb, wu_ref[:, cols].astype(BF16))
            acc = acc + _dot((_silu(gate) * up).astype(BF16), wd_ref[cols, :].astype(BF16))
            if 2 <= c < pieces + 2:
                epilogue_piece(c - 2, acc[0:1, :])
        acc_keep[...] = acc

    @pl.when(i == last)
    def _():
        for k in range(pieces):
            epilogue_piece(k)


def _resident(shape):
    return pl.BlockSpec(shape, lambda *_: (0,) * len(shape), pipeline_mode=pl.Buffered(1))


def _ffn_ln(x2d, wg, wu, wd, g, b, name):
    t = x2d.shape[0]
    tiles = t // FFN_ROWS
    return pl.pallas_call(
        _ffn_ln_kernel,
        grid=(tiles + 1,),
        in_specs=[
            pl.BlockSpec((FFN_ROWS, D_MODEL), lambda i: (jnp.minimum(i, tiles - 1), 0)),
            pl.BlockSpec((FFN_ROWS, D_MODEL), lambda i: (jnp.maximum(i - 1, 0), 0)),
            _resident((D_MODEL, D_FF)),
            _resident((D_MODEL, D_FF)),
            _resident(wd.shape),
            _resident((1, D_MODEL)),
            _resident((1, D_MODEL)),
        ],
        out_specs=pl.BlockSpec((FFN_ROWS, D_MODEL), lambda i: (jnp.maximum(i - 1, 0), 0)),
        out_shape=jax.ShapeDtypeStruct((t, D_MODEL), F32),
        scratch_shapes=[
            pltpu.VMEM((FFN_ROWS, D_MODEL), F32),
        ],
        compiler_params=pltpu.CompilerParams(
            dimension_semantics=("arbitrary",), vmem_limit_bytes=FFN_VMEM_LIMIT),
        name=name,
    )(x2d, x2d, wg, wu, wd, g, b)


def _mixer_masks():
    r = np.arange(MIX_ROWS)[:, None]
    c = np.arange(MIX_ROWS)[None, :]
    block_diag = (r // CHUNK == c // CHUNK)
    in_win = np.stack([(c <= r) & (c > r - w) for w in POOL_WINDOWS])
    rt = np.arange(MAX_WINDOW)[:, None]
    ct = np.arange(MAX_WINDOW)[None, :]
    in_tail = np.stack([(ct - MAX_WINDOW > rt - w) for w in POOL_WINDOWS])
    head_pair_ones = (r // DN_HEAD_DIM == c // DN_HEAD_DIM)
    as_bf16 = lambda m: jnp.asarray(m.astype(np.float32), dtype=BF16)
    return as_bf16(block_diag), as_bf16(in_win), as_bf16(in_tail), as_bf16(head_pair_ones)


def _run_skewed(pipelines, lag):
    pending = list(pipelines)
    live = []
    step = 0
    while pending or live:
        if pending and step % lag == 0:
            live.append(pending.pop(0))
        for p in list(live):
            try:
                next(p)
            except StopIteration:
                live.remove(p)
        step += 1


def _mixer_kernel(h_ref, win_ref, convw_ref, prm_ref, dng_ref,
                  dnproj_ref, poolw_ref, pscale_ref, pproj_ref, wout_ref, lng_ref, lnb_ref,
                  bdmask_ref, inwin_ref, intail_ref, ones2_ref,
                  o_ref, state_ref, xtail_ref, ptail_ref):
    ts = MIX_ROWS
    tile = pl.program_id(1)

    @pl.when(tile == 0)
    def _():
        state_ref[...] = jnp.zeros(state_ref.shape, F32)
        xtail_ref[...] = jnp.zeros(xtail_ref.shape, F32)
        ptail_ref[...] = jnp.zeros(ptail_ref.shape, F32)

    ones2 = ones2_ref[...]
    bdmask = bdmask_ref[...]
    prow = lax.broadcasted_iota(jnp.int32, (CHUNK, ts), 0)
    plane = lax.broadcasted_iota(jnp.int32, (CHUNK, ts), 1)
    pcol = plane % CHUNK
    in_block = [plane // CHUNK == c for c in range(MIX_CHUNKS)]
    causal_p = prow >= pcol
    strict_p = prow > pcol
    eye_p = (prow == pcol).astype(F32)
    low_lanes = lax.broadcasted_iota(jnp.int32, (CHUNK, LANES), 1) < CHUNK
    t_abs = tile * ts + lax.broadcasted_iota(jnp.int32, (ts, POOL_GROUP_DIM), 0)

    def column_broadcast(row):
        return jnp.broadcast_to(row, (DN_HEAD_DIM, ts)).T

    def pack(full):
        out = full[0:CHUNK]
        for c in range(1, MIX_CHUNKS):
            out = jnp.where(in_block[c], full[c * CHUNK:(c + 1) * CHUNK], out)
        return out

    def block_diag(packed):
        return jnp.concatenate([packed.astype(BF16)] * MIX_CHUNKS, axis=0) * bdmask

    def group_stages(first):
        ng = MIX_GROUP
        seqs = range(ng)
        rows_of = lambda s: slice(s * ts, (s + 1) * ts)
        h = h_ref[first:first + ng].reshape(ng * ts, D_MODEL)
        hb = h.astype(BF16)

        def project(first_col, width):
            return _dot_nt(hb, win_ref[first_col:first_col + width, :].astype(BF16))

        def write_strength_and_decay():
            ba_t = _dot_nt(win_ref[W_BA0:W_BA0 + SUBLANES, :].astype(BF16), hb)
            beta = _sigmoid(ba_t)
            a_shift = ba_t + prm_ref[1]
            softplus = jnp.maximum(a_shift, 0.0) + jnp.log1p(jnp.exp(-jnp.abs(a_shift)))
            g = -jnp.exp(prm_ref[0]) * softplus
            lane_in_chunk = lax.broadcasted_iota(jnp.int32, (SUBLANES, ng * ts), 1) % CHUNK
            shift = 1
            while shift < CHUNK:
                rolled = jnp.concatenate(
                    [pltpu.roll(g[:, i * LANES:(i + 1) * LANES], shift, 1) for i in range(ng * ts // LANES)],
                    axis=1)
                g = g + jnp.where(lane_in_chunk >= shift, rolled, 0.0)
                shift *= 2
            return beta, g

        def gate_group(j):
            return _sigmoid(project(W_GATE0 + j * DN_WIDTH, DN_WIDTH))

        qkv = []
        gate_sig = []
        for part in range(3):
            cols = slice(part * DN_WIDTH, (part + 1) * DN_WIDTH)
            raw = project(part * DN_WIDTH, DN_WIDTH)
            cw = convw_ref[:, cols]
            conv = []
            for s in seqs:
                raw_s = raw[rows_of(s)]
                ext = jnp.concatenate([xtail_ref[first + s, :, cols], raw_s], axis=0)
                xtail_ref[first + s, :, cols] = raw_s[ts - SUBLANES:, :]
                y = pltpu.roll(ext, CONV_WIDTH - 1, 0)[SUBLANES:] * cw[0:1]
                for tap in range(1, CONV_WIDTH - 1):
                    y = y + pltpu.roll(ext, CONV_WIDTH - 1 - tap, 0)[SUBLANES:] * cw[tap:tap + 1]
                conv.append(y + raw_s * cw[CONV_WIDTH - 1:CONV_WIDTH])
            qkv.append(_silu(jnp.concatenate(conv, axis=0)))
            gate_sig.append(gate_group(part))
            if part == 1:
                beta_t, gr = write_strength_and_decay()
            yield
        q_all = qkv[0] * (lax.rsqrt(_head_sums(qkv[0] * qkv[0], ones2) + RMS_EPS) * (DN_HEAD_DIM ** -0.5))
        k_all = qkv[1] * lax.rsqrt(_head_sums(qkv[1] * qkv[1], ones2) + RMS_EPS)
        v_all = qkv[2]

        p = project(W_P0, POOL_WIDTH)
        pooled = []
        for s in seqs:
            p_s = p[rows_of(s)]
            tail = ptail_ref[first + s]
            ptail_ref[first + s] = p_s[ts - MAX_WINDOW:, :]
            pooled_s = []
            for gi, win in enumerate(POOL_WINDOWS):
                lo = gi * POOL_GROUP_DIM
                pg = p_s[:, lo:lo + POOL_GROUP_DIM]
                wsum = _dot(inwin_ref[gi], pg.astype(BF16))
                top = wsum[:MAX_WINDOW] + _dot(intail_ref[gi], tail[:, lo:lo + POOL_GROUP_DIM].astype(BF16))
                wsum = jnp.concatenate([top, wsum[MAX_WINDOW:]], axis=0)
                count = jnp.minimum(t_abs + 1, win).astype(F32)
                pooled_s.append(wsum / count - pg)
            pooled.append(jnp.concatenate(pooled_s, axis=1))
        pooled = jnp.concatenate(pooled, axis=0).astype(BF16)
        pooled = jnp.concatenate(
            [_dot(pooled[:, gi * POOL_GROUP_DIM:(gi + 1) * POOL_GROUP_DIM], poolw_ref[gi].astype(BF16))
             for gi in range(POOL_GROUPS)], axis=1) * pscale_ref[...]
        y_pool = _dot(pooled.astype(BF16), pproj_ref[...].astype(BF16))
        z = project(W_Z0, DN_WIDTH)
        gate_sig.append(gate_group(3))
        yield

        chains = [(s, hd) for s in seqs for hd in range(DN_HEADS)]
        ids = range(len(chains))
        qn, kn_b, k_beta, v_beta, eg_b, kd_t, decay_p = [], [], [], [], [], [], []
        for s, hd in chains:
            lanes = slice(hd * DN_HEAD_DIM, (hd + 1) * DN_HEAD_DIM)
            qn.append(q_all[rows_of(s), lanes])
            kn = k_all[rows_of(s), lanes]
            v = v_all[rows_of(s), lanes]
            beta_b = column_broadcast(beta_t[hd:hd + 1, rows_of(s)])
            gr_row = gr[DN_HEADS + hd:DN_HEADS + hd + 1, rows_of(s)]
            gc_b = column_broadcast(gr_row)
            gc_p = jnp.concatenate(
                [jnp.where(low_lanes, gc_b[2 * i * CHUNK:(2 * i + 1) * CHUNK],
                           gc_b[(2 * i + 1) * CHUNK:(2 * i + 2) * CHUNK])
                 for i in range(MIX_CHUNKS // 2)], axis=1)
            decay_p.append(jnp.exp(jnp.where(causal_p, gc_p - gr_row, -jnp.inf)))
            eg_b.append(jnp.exp(gc_b))
            gl_b = jnp.concatenate(
                [jnp.broadcast_to(gc_b[c * CHUNK + CHUNK - 1:(c + 1) * CHUNK, :], (CHUNK, DN_HEAD_DIM))
                 for c in range(MIX_CHUNKS)], axis=0)
            k_beta.append(kn * beta_b)
            v_beta.append(v * beta_b)
            kn_b.append(kn.astype(BF16))
            kd_t.append((kn * jnp.exp(gl_b - gc_b)).T.astype(BF16))
        gram = [_dot_nt(jnp.concatenate([k_beta[i], qn[i]], axis=0).astype(BF16), kn_b[i]) for i in ids]
        yield
        lower = [pack(gram[i][:ts]) * jnp.where(strict_p, decay_p[i], 0.0) for i in ids]
        qk = [block_diag(pack(gram[i][ts:]) * decay_p[i]) for i in ids]
        t_mat = [eye_p - lower[i] for i in ids]
        power = [_dot(lower[i].astype(BF16), block_diag(lower[i])) for i in ids]
        yield
        for _ in range(4):
            both = [_dot(jnp.concatenate([power[i], t_mat[i]], axis=0).astype(BF16), block_diag(power[i]))
                    for i in ids]
            power = [both[i][:CHUNK] for i in ids]
            t_mat = [t_mat[i] + both[i][CHUNK:] for i in ids]
            yield
        t_mat = [t_mat[i] + _dot(t_mat[i].astype(BF16), block_diag(power[i])) for i in ids]
        yield
        uw = [_dot(block_diag(t_mat[i]),
                   jnp.concatenate([v_beta[i], k_beta[i] * eg_b[i]], axis=1).astype(BF16)) for i in ids]
        q_dec = [qn[i] * eg_b[i] for i in ids]
        state = [state_ref[first + s, hd] for s, hd in chains]
        yield
        kd_uw = []
        for i in ids:
            per_chunk = []
            for c in range(MIX_CHUNKS):
                uw_pad = jnp.concatenate(
                    [uw[i][cc * CHUNK:(cc + 1) * CHUNK] if cc == c else jnp.zeros((CHUNK, 2 * DN_HEAD_DIM), F32)
                     for cc in range(MIX_CHUNKS)], axis=0).astype(BF16)
                per_chunk.append(_dot(kd_t[i], uw_pad))
            kd_uw.append(per_chunk)
        yield
        o_inter = [[] for _ in ids]
        v_new_all = [[] for _ in ids]
        for c in range(MIX_CHUNKS):
            r0 = c * CHUNK
            for i in ids:
                u_c = uw[i][r0:r0 + CHUNK, :DN_HEAD_DIM]
                w_c = uw[i][r0:r0 + CHUNK, DN_HEAD_DIM:]
                lhs = jnp.concatenate(
                    [kd_uw[i][c][:, DN_HEAD_DIM:], w_c, q_dec[i][r0:r0 + CHUNK]], axis=0).astype(BF16)
                prod = _dot(lhs, state[i].astype(BF16))
                v_new_all[i].append(u_c - prod[DN_HEAD_DIM:DN_HEAD_DIM + CHUNK])
                o_inter[i].append(prod[DN_HEAD_DIM + CHUNK:])
                g_last = eg_b[i][r0 + CHUNK - 1:r0 + CHUNK, :]
                state[i] = state[i] * g_last - prod[:DN_HEAD_DIM] + kd_uw[i][c][:, :DN_HEAD_DIM]
            yield
        o_chain = []
        for i, (s, hd) in enumerate(chains):
            state_ref[first + s, hd] = state[i]
            v_new_full = jnp.concatenate(v_new_all[i], axis=0).astype(BF16)
            o_chain.append(jnp.concatenate(o_inter[i], axis=0) + _dot(qk[i], v_new_full))
        o_all = jnp.concatenate(
            [jnp.concatenate(o_chain[s * DN_HEADS:(s + 1) * DN_HEADS], axis=1) for s in seqs], axis=0)
        o_all = o_all * lax.rsqrt(_head_sums(o_all * o_all, ones2) * (1.0 / DN_HEAD_DIM) + RMS_EPS)
        o_all = o_all * dng_ref[...] * _silu(z)
        y_dn = _dot(o_all.astype(BF16), dnproj_ref[...].astype(BF16))
        yield

        merged = (jnp.concatenate(gate_sig[:2], axis=1) * y_dn
                  + jnp.concatenate(gate_sig[2:], axis=1) * y_pool)
        m = _dot(merged.astype(BF16), wout_ref[...].astype(BF16))
        out = _layer_norm(ALPHA * h + m, lng_ref[...], lnb_ref[...])
        o_ref[first:first + ng] = out.reshape(ng, ts, D_MODEL)
        yield

    _run_skewed([group_stages(first) for first in range(0, MIX_BATCH, MIX_GROUP)], MIX_LAG)


def _mixer_ln(h3d, win_t, convw, prm, dng, dnproj, poolw, pscale, pproj, wout, lng, lnb):
    b, s, _ = h3d.shape
    assert win_t.shape == (W_GATE0 + 2 * D_MODEL, D_MODEL)
    operands = (win_t, convw, prm, dng, dnproj, poolw, pscale, pproj, wout, lng, lnb, *_mixer_masks())
    block = (MIX_BATCH, MIX_ROWS, D_MODEL)
    return pl.pallas_call(
        _mixer_kernel,
        grid=(b // MIX_BATCH, s // MIX_ROWS),
        in_specs=[pl.BlockSpec(block, lambda i, j: (i, j, 0))] + [_resident(op.shape) for op in operands],
        out_specs=pl.BlockSpec(block, lambda i, j: (i, j, 0)),
        out_shape=jax.ShapeDtypeStruct(h3d.shape, F32),
        scratch_shapes=[
            pltpu.VMEM((MIX_BATCH, DN_HEADS, DN_HEAD_DIM, DN_HEAD_DIM), F32),
            pltpu.VMEM((MIX_BATCH, SUBLANES, QKV_COLS), F32),
            pltpu.VMEM((MIX_BATCH, MAX_WINDOW, POOL_WIDTH), F32),
        ],
        compiler_params=pltpu.CompilerParams(
            dimension_semantics=("arbitrary", "arbitrary"), vmem_limit_bytes=MIX_VMEM_LIMIT),
        name="mixer_ln",
    )(h3d, *operands)


def _row(v):
    return v.reshape(1, -1).astype(F32)


def kernel(x, ffn_pre_w_gate, ffn_pre_w_up, ffn_pre_w_down, norm_pre_g, norm_pre_b, mix_w_in, mix_conv_w, dn_a_log, dn_dt_bias, dn_norm_g, dn_w_proj, pool_w, pool_scale, pool_w_proj, mix_w_out, norm_mix_g, norm_mix_b, ffn_post_w_gate, ffn_post_w_up, ffn_post_w_down, norm_post_g, norm_post_b):
    bsz, seq, _ = x.shape
    h = x
    for l in range(DEPTH):
        h = _ffn_ln(h.reshape(bsz * seq, D_MODEL),
                    ffn_pre_w_gate[l], ffn_pre_w_up[l], ffn_pre_w_down[l], _row(norm_pre_g[l]), _row(norm_pre_b[l]),
                    "ffn_ln_pre")

        lanes = MIX_GROUP * MIX_ROWS
        prm = jnp.zeros((2, SUBLANES, lanes), F32)
        prm = prm.at[0, DN_HEADS:2 * DN_HEADS].set(jnp.broadcast_to(dn_a_log[l][:, None], (DN_HEADS, lanes)))
        prm = prm.at[1, DN_HEADS:2 * DN_HEADS].set(jnp.broadcast_to(dn_dt_bias[l][:, None], (DN_HEADS, lanes)))
        h = _mixer_ln(
            h.reshape(bsz, seq, D_MODEL),
            jnp.swapaxes(mix_w_in[l], 0, 1),
            mix_conv_w[l].astype(F32), prm, _row(jnp.tile(dn_norm_g[l], DN_HEADS)),
            dn_w_proj[l], pool_w[l], _row(pool_scale[l]), pool_w_proj[l], mix_w_out[l],
            _row(norm_mix_g[l]), _row(norm_mix_b[l]))

        h = _ffn_ln(h.reshape(bsz * seq, D_MODEL),
                    ffn_post_w_gate[l], ffn_post_w_up[l], ffn_post_w_down[l], _row(norm_post_g[l]), _row(norm_post_b[l]),
                    "ffn_ln_post")
    return h.reshape(bsz, seq, D_MODEL)
```

```python
import functools

import numpy as np

import jax
import jax.numpy as jnp
from jax import lax
from jax.experimental import pallas as pl
from jax.experimental.pallas import tpu as pltpu

F32 = jnp.float32
BF16 = jnp.bfloat16

D_MODEL = 1024
D_FF = 2816
DN_HEADS = 4
DN_HEAD_DIM = 128
DN_WIDTH = DN_HEADS * DN_HEAD_DIM
CONV_WIDTH = 4
CHUNK = 64
POOL_WINDOWS = (2, 4, 8, 16)
POOL_GROUPS = 4
POOL_GROUP_DIM = 128
POOL_WIDTH = POOL_GROUPS * POOL_GROUP_DIM
QKV_COLS = 3 * DN_WIDTH
W_Z0 = QKV_COLS
W_BA0 = W_Z0 + DN_WIDTH
W_P0 = W_BA0 + 2 * DN_HEADS
W_GATE0 = W_P0 + POOL_WIDTH
DEPTH = 1
ALPHA = (2.0 * DEPTH) ** 0.25
LN_EPS = 1e-5
RMS_EPS = 1e-6

SUBLANES = 8
LANES = 128
MIB = 1024 * 1024

FFN_ROWS = 512
FFN_EPILOGUE_ROWS = 64
FFN_COLS = 256
MIX_ROWS = 256
MIX_CHUNKS = MIX_ROWS // CHUNK
MIX_BATCH = 2
MIX_GROUP = 2
MIX_LAG = 6
MIX_EPILOGUE_ROWS = 64
MAX_WINDOW = max(POOL_WINDOWS)
FFN_VMEM_LIMIT = 58 * MIB
MIX_VMEM_LIMIT = 56 * MIB


def _dot(a, b, **kw):
    return jnp.dot(a, b, preferred_element_type=F32, **kw)


def _dot_nt(a, b):
    return lax.dot_general(a, b, (((1,), (1,)), ((), ())), preferred_element_type=F32)


def _layer_norm(y, g, b):
    mu = jnp.mean(y, axis=-1, keepdims=True)
    yc = y - mu
    var = jnp.mean(yc * yc, axis=-1, keepdims=True)
    return yc * lax.rsqrt(var + LN_EPS) * g + b


def _sigmoid(x):
    return 0.5 * jnp.tanh(0.5 * x) + 0.5


def _silu(x):
    half = 0.5 * x
    return half + half * jnp.tanh(half)


def _head_sums(sq, ones2):
    sq = sq.astype(BF16)
    width = ones2.shape[0]
    return jnp.concatenate(
        [_dot(sq[:, i * width:(i + 1) * width], ones2) for i in range(sq.shape[1] // width)], axis=1)


def _issued_after(value, anchor):
    never = (anchor == anchor) & (anchor != anchor)
    return value + jnp.where(never, 1.0, 0.0)


def _ffn_ln_kernel(x_ref, x_prev_ref, wg_ref, wu_ref, wd_ref, g_ref, b_ref, o_ref, acc_keep):
    i = pl.program_id(0)
    last = pl.num_programs(0) - 1
    pieces = FFN_ROWS // FFN_EPILOGUE_ROWS
    n_chunks = D_FF // FFN_COLS
    assert pieces + 2 <= n_chunks

    def epilogue_piece(k, anchor=None):
        rows = slice(k * FFN_EPILOGUE_ROWS, (k + 1) * FFN_EPILOGUE_ROWS)
        x_rows = x_prev_ref[rows, :]
        if anchor is not None:
            x_rows = _issued_after(x_rows, anchor)
        y = ALPHA * x_rows + 0.5 * acc_keep[rows, :]
        o_ref[rows, :] = _layer_norm(y, g_ref[...], b_ref[...])

    @pl.when(i == 0)
    def _():
        acc_keep[...] = jnp.zeros(acc_keep.shape, F32)

    @pl.when(i < last)
    def _():
        x = x_ref[...]
        xb = x.astype(BF16)
        acc = jnp.zeros(x.shape, F32)
        for c in range(n_chunks):
            cols = slice(c * FFN_COLS, (c + 1) * FFN_COLS)
            gate = _dot(xb, wg_ref[:, cols].astype(BF16))
            up = _dot(xb, wu_ref[:, cols].astype(BF16))
            acc = acc + _dot((_silu(gate) * up).astype(BF16), wd_ref[cols, :].astype(BF16))
            if 2 <= c < pieces + 2:
                epilogue_piece(c - 2, acc[0:1, :])
        acc_keep[...] = acc

    @pl.when(i == last)
    def _():
        for k in range(pieces):
            epilogue_piece(k)


def _resident(shape):
    return pl.BlockSpec(shape, lambda *_: (0,) * len(shape), pipeline_mode=pl.Buffered(1))


def _ffn_ln(x2d, wg, wu, wd, g, b, name):
    t = x2d.shape[0]
    tiles = t // FFN_ROWS
    return pl.pallas_call(
        _ffn_ln_kernel,
        grid=(tiles + 1,),
        in_specs=[
            pl.BlockSpec((FFN_ROWS, D_MODEL), lambda i: (jnp.minimum(i, tiles - 1), 0)),
            pl.BlockSpec((FFN_ROWS, D_MODEL), lambda i: (jnp.maximum(i - 1, 0), 0)),
            _resident((D_MODEL, D_FF)),
            _resident((D_MODEL, D_FF)),
            _resident(wd.shape),
            _resident((1, D_MODEL)),
            _resident((1, D_MODEL)),
        ],
        out_specs=pl.BlockSpec((FFN_ROWS, D_MODEL), lambda i: (jnp.maximum(i - 1, 0), 0)),
        out_shape=jax.ShapeDtypeStruct((t, D_MODEL), F32),
        scratch_shapes=[
            pltpu.VMEM((FFN_ROWS, D_MODEL), F32),
        ],
        compiler_params=pltpu.CompilerParams(
            dimension_semantics=("arbitrary",), vmem_limit_bytes=FFN_VMEM_LIMIT),
        name=name,
    )(x2d, x2d, wg, wu, wd, g, b)


def _mixer_masks():
    r = np.arange(MIX_ROWS)[:, None]
    c = np.arange(MIX_ROWS)[None, :]
    block_diag = (r // CHUNK == c // CHUNK)
    in_win = np.stack([(c <= r) & (c > r - w) for w in POOL_WINDOWS])
    rt = np.arange(MAX_WINDOW)[:, None]
    ct = np.arange(MAX_WINDOW)[None, :]
    in_tail = np.stack([(ct - MAX_WINDOW > rt - w) for w in POOL_WINDOWS])
    head_pair_ones = (r // DN_HEAD_DIM == c // DN_HEAD_DIM)
    as_bf16 = lambda m: jnp.asarray(m.astype(np.float32), dtype=BF16)
    return as_bf16(block_diag), as_bf16(in_win), as_bf16(in_tail), as_bf16(head_pair_ones)


def _run_skewed(pipelines, lag, on_anchor):
    pending = list(pipelines)
    live = []
    step = 0
    while pending or live:
        if pending and step % lag == 0:
            live.append(pending.pop(0))
        for p in list(live):
            try:
                anchor = next(p)
            except StopIteration:
                live.remove(p)
            else:
                if anchor is not None:
                    on_anchor(anchor)
        step += 1


def _mixer_kernel(h_ref, win_ref, convw_ref, prm_ref, dng_ref,
                  dnproj_ref, poolw_ref, pscale_ref, pproj_ref, wout_ref, lng_ref, lnb_ref,
                  bdmask_ref, inwin_ref, intail_ref, ones2_ref,
                  o_ref, state_ref, xtail_ref, ptail_ref, y_keep, *, tiles):
    ts = MIX_ROWS
    step = pl.program_id(0)
    last = pl.num_programs(0) - 1
    tile = step % tiles
    pieces = MIX_BATCH * ts // MIX_EPILOGUE_ROWS

    def epilogue_piece(k, anchor=None):
        seq, lo = divmod(k * MIX_EPILOGUE_ROWS, ts)
        y = y_keep[k * MIX_EPILOGUE_ROWS:(k + 1) * MIX_EPILOGUE_ROWS, :]
        if anchor is not None:
            y = _issued_after(y, anchor)
        o_ref[seq, lo:lo + MIX_EPILOGUE_ROWS, :] = _layer_norm(y, lng_ref[...], lnb_ref[...])

    @pl.when(step == 0)
    def _():
        y_keep[...] = jnp.zeros(y_keep.shape, F32)

    @pl.when(step == last)
    def _():
        for k in range(pieces):
            epilogue_piece(k)

    @pl.when(tile == 0)
    def _():
        state_ref[...] = jnp.zeros(state_ref.shape, F32)
        xtail_ref[...] = jnp.zeros(xtail_ref.shape, F32)
        ptail_ref[...] = jnp.zeros(ptail_ref.shape, F32)

    pl.when(step < last)(functools.partial(
        _mixer_tile, h_ref, win_ref, convw_ref, prm_ref, dng_ref, dnproj_ref, poolw_ref, pscale_ref,
        pproj_ref, wout_ref, bdmask_ref, inwin_ref, intail_ref, ones2_ref, state_ref, xtail_ref,
        ptail_ref, y_keep, tile, epilogue_piece, pieces))


def _mixer_tile(h_ref, win_ref, convw_ref, prm_ref, dng_ref, dnproj_ref, poolw_ref, pscale_ref,
                pproj_ref, wout_ref, bdmask_ref, inwin_ref, intail_ref, ones2_ref, state_ref, xtail_ref,
                ptail_ref, y_keep, tile, epilogue_piece, pieces):
    ts = MIX_ROWS

    ones2 = ones2_ref[...]
    bdmask = bdmask_ref[...]
    prow = lax.broadcasted_iota(jnp.int32, (CHUNK, ts), 0)
    plane = lax.broadcasted_iota(jnp.int32, (CHUNK, ts), 1)
    pcol = plane % CHUNK
    in_block = [plane // CHUNK == c for c in range(MIX_CHUNKS)]
    causal_p = prow >= pcol
    strict_p = prow > pcol
    eye_p = (prow == pcol).astype(F32)
    low_lanes = lax.broadcasted_iota(jnp.int32, (CHUNK, LANES), 1) < CHUNK
    t_abs = tile * ts + lax.broadcasted_iota(jnp.int32, (ts, POOL_GROUP_DIM), 0)

    def column_broadcast(row):
        return jnp.broadcast_to(row, (DN_HEAD_DIM, ts)).T

    def pack(full):
        out = full[0:CHUNK]
        for c in range(1, MIX_CHUNKS):
            out = jnp.where(in_block[c], full[c * CHUNK:(c + 1) * CHUNK], out)
        return out

    def block_diag(packed):
        return jnp.concatenate([packed.astype(BF16)] * MIX_CHUNKS, axis=0) * bdmask

    def group_stages(first):
        ng = MIX_GROUP
        seqs = range(ng)
        rows_of = lambda s: slice(s * ts, (s + 1) * ts)
        h = h_ref[first:first + ng].reshape(ng * ts, D_MODEL)
        hb = h.astype(BF16)

        def project(first_col, width):
            return _dot_nt(hb, win_ref[first_col:first_col + width, :].astype(BF16))

        def write_strength_and_decay():
            ba_t = _dot_nt(win_ref[W_BA0:W_BA0 + SUBLANES, :].astype(BF16), hb)
            beta = _sigmoid(ba_t)
            a_shift = ba_t + prm_ref[1]
            softplus = jnp.maximum(a_shift, 0.0) + jnp.log1p(jnp.exp(-jnp.abs(a_shift)))
            g = -jnp.exp(prm_ref[0]) * softplus
            lane_in_chunk = lax.broadcasted_iota(jnp.int32, (SUBLANES, ng * ts), 1) % CHUNK
            shift = 1
            while shift < CHUNK:
                rolled = jnp.concatenate(
                    [pltpu.roll(g[:, i * LANES:(i + 1) * LANES], shift, 1) for i in range(ng * ts // LANES)],
                    axis=1)
                g = g + jnp.where(lane_in_chunk >= shift, rolled, 0.0)
                shift *= 2
            return beta, g

        def gate_group(j):
            return _sigmoid(project(W_GATE0 + j * DN_WIDTH, DN_WIDTH))

        qkv = []
        gate_sig = []
        for part in range(3):
            cols = slice(part * DN_WIDTH, (part + 1) * DN_WIDTH)
            raw = project(part * DN_WIDTH, DN_WIDTH)
            cw = convw_ref[:, cols]
            conv = []
            for s in seqs:
                raw_s = raw[rows_of(s)]
                ext = jnp.concatenate([xtail_ref[first + s, :, cols], raw_s], axis=0)
                xtail_ref[first + s, :, cols] = raw_s[ts - SUBLANES:, :]
                y = pltpu.roll(ext, CONV_WIDTH - 1, 0)[SUBLANES:] * cw[0:1]
                for tap in range(1, CONV_WIDTH - 1):
                    y = y + pltpu.roll(ext, CONV_WIDTH - 1 - tap, 0)[SUBLANES:] * cw[tap:tap + 1]
                conv.append(y + raw_s * cw[CONV_WIDTH - 1:CONV_WIDTH])
            qkv.append(_silu(jnp.concatenate(conv, axis=0)))
            gate_sig.append(gate_group(part))
            if part == 1:
                beta_t, gr = write_strength_and_decay()
            yield
        q_all = qkv[0] * (lax.rsqrt(_head_sums(qkv[0] * qkv[0], ones2) + RMS_EPS) * (DN_HEAD_DIM ** -0.5))
        k_all = qkv[1] * lax.rsqrt(_head_sums(qkv[1] * qkv[1], ones2) + RMS_EPS)
        v_all = qkv[2]

        p = project(W_P0, POOL_WIDTH)
        pooled = []
        for s in seqs:
            p_s = p[rows_of(s)]
            tail = ptail_ref[first + s]
            ptail_ref[first + s] = p_s[ts - MAX_WINDOW:, :]
            pooled_s = []
            for gi, win in enumerate(POOL_WINDOWS):
                lo = gi * POOL_GROUP_DIM
                pg = p_s[:, lo:lo + POOL_GROUP_DIM]
                wsum = _dot(inwin_ref[gi], pg.astype(BF16))
                top = wsum[:MAX_WINDOW] + _dot(intail_ref[gi], tail[:, lo:lo + POOL_GROUP_DIM].astype(BF16))
                wsum = jnp.concatenate([top, wsum[MAX_WINDOW:]], axis=0)
                count = jnp.minimum(t_abs + 1, win).astype(F32)
                pooled_s.append(wsum / count - pg)
            pooled.append(jnp.concatenate(pooled_s, axis=1))
        pooled = jnp.concatenate(pooled, axis=0).astype(BF16)
        pooled = jnp.concatenate(
            [_dot(pooled[:, gi * POOL_GROUP_DIM:(gi + 1) * POOL_GROUP_DIM], poolw_ref[gi].astype(BF16))
             for gi in range(POOL_GROUPS)], axis=1) * pscale_ref[...]
        y_pool = _dot(pooled.astype(BF16), pproj_ref[...].astype(BF16))
        z = project(W_Z0, DN_WIDTH)
        gate_sig.append(gate_group(3))
        yield

        chains = [(s, hd) for s in seqs for hd in range(DN_HEADS)]
        ids = range(len(chains))
        qn, kn_b, k_beta, v_beta, eg_b, kd_t, decay_p = [], [], [], [], [], [], []
        for s, hd in chains:
            lanes = slice(hd * DN_HEAD_DIM, (hd + 1) * DN_HEAD_DIM)
            qn.append(q_all[rows_of(s), lanes])
            kn = k_all[rows_of(s), lanes]
            v = v_all[rows_of(s), lanes]
            beta_b = column_broadcast(beta_t[hd:hd + 1, rows_of(s)])
            gr_row = gr[DN_HEADS + hd:DN_HEADS + hd + 1, rows_of(s)]
            gc_b = column_broadcast(gr_row)
            gc_p = jnp.concatenate(
                [jnp.where(low_lanes, gc_b[2 * i * CHUNK:(2 * i + 1) * CHUNK],
                           gc_b[(2 * i + 1) * CHUNK:(2 * i + 2) * CHUNK])
                 for i in range(MIX_CHUNKS // 2)], axis=1)
            decay_p.append(jnp.exp(jnp.where(causal_p, gc_p - gr_row, -jnp.inf)))
            eg_b.append(jnp.exp(gc_b))
            gl_b = jnp.concatenate(
                [jnp.broadcast_to(gc_b[c * CHUNK + CHUNK - 1:(c + 1) * CHUNK, :], (CHUNK, DN_HEAD_DIM))
                 for c in range(MIX_CHUNKS)], axis=0)
            k_beta.append(kn * beta_b)
            v_beta.append(v * beta_b)
            kn_b.append(kn.astype(BF16))
            kd_t.append((kn * jnp.exp(gl_b - gc_b)).T.astype(BF16))
        gram = [_dot_nt(jnp.concatenate([k_beta[i], qn[i]], axis=0).astype(BF16), kn_b[i]) for i in ids]
        yield
        lower = [pack(gram[i][:ts]) * jnp.where(strict_p, decay_p[i], 0.0) for i in ids]
        qk = [block_diag(pack(gram[i][ts:]) * decay_p[i]) for i in ids]
        t_mat = [eye_p - lower[i] for i in ids]
        power = [_dot(lower[i].astype(BF16), block_diag(lower[i])) for i in ids]
        yield
        for _ in range(4):
            both = [_dot(jnp.concatenate([power[i], t_mat[i]], axis=0).astype(BF16), block_diag(power[i]))
                    for i in ids]
            power = [both[i][:CHUNK] for i in ids]
            t_mat = [t_mat[i] + both[i][CHUNK:] for i in ids]
            yield both[0][0:1, 0:1]
        t_mat = [t_mat[i] + _dot(t_mat[i].astype(BF16), block_diag(power[i])) for i in ids]
        yield t_mat[0][0:1, 0:1]
        uw = [_dot(block_diag(t_mat[i]),
                   jnp.concatenate([v_beta[i], k_beta[i] * eg_b[i]], axis=1).astype(BF16)) for i in ids]
        q_dec = [qn[i] * eg_b[i] for i in ids]
        state = [state_ref[first + s, hd] for s, hd in chains]
        yield
        kd_uw = []
        for i in ids:
            per_chunk = []
            for c in range(MIX_CHUNKS):
                uw_pad = jnp.concatenate(
                    [uw[i][cc * CHUNK:(cc + 1) * CHUNK] if cc == c else jnp.zeros((CHUNK, 2 * DN_HEAD_DIM), F32)
                     for cc in range(MIX_CHUNKS)], axis=0).astype(BF16)
                per_chunk.append(_dot(kd_t[i], uw_pad))
            kd_uw.append(per_chunk)
        yield
        o_inter = [[] for _ in ids]
        v_new_all = [[] for _ in ids]
        for c in range(MIX_CHUNKS):
            r0 = c * CHUNK
            for i in ids:
                u_c = uw[i][r0:r0 + CHUNK, :DN_HEAD_DIM]
                w_c = uw[i][r0:r0 + CHUNK, DN_HEAD_DIM:]
                lhs = jnp.concatenate(
                    [kd_uw[i][c][:, DN_HEAD_DIM:], w_c, q_dec[i][r0:r0 + CHUNK]], axis=0).astype(BF16)
                prod = _dot(lhs, state[i].astype(BF16))
                v_new_all[i].append(u_c - prod[DN_HEAD_DIM:DN_HEAD_DIM + CHUNK])
                o_inter[i].append(prod[DN_HEAD_DIM + CHUNK:])
                g_last = eg_b[i][r0 + CHUNK - 1:r0 + CHUNK, :]
                state[i] = state[i] * g_last - prod[:DN_HEAD_DIM] + kd_uw[i][c][:, :DN_HEAD_DIM]
            yield state[0][0:1, 0:1]
        o_chain = []
        for i, (s, hd) in enumerate(chains):
            state_ref[first + s, hd] = state[i]
            v_new_full = jnp.concatenate(v_new_all[i], axis=0).astype(BF16)
            o_chain.append(jnp.concatenate(o_inter[i], axis=0) + _dot(qk[i], v_new_full))
        o_all = jnp.concatenate(
            [jnp.concatenate(o_chain[s * DN_HEADS:(s + 1) * DN_HEADS], axis=1) for s in seqs], axis=0)
        o_all = o_all * lax.rsqrt(_head_sums(o_all * o_all, ones2) * (1.0 / DN_HEAD_DIM) + RMS_EPS)
        o_all = o_all * dng_ref[...] * _silu(z)
        y_dn = _dot(o_all.astype(BF16), dnproj_ref[...].astype(BF16))
        yield

        merged = (jnp.concatenate(gate_sig[:2], axis=1) * y_dn
                  + jnp.concatenate(gate_sig[2:], axis=1) * y_pool)
        m = _dot(merged.astype(BF16), wout_ref[...].astype(BF16))
        pre_norm.append((first, ALPHA * h + m))
        yield

    pre_norm = []
    todo = list(range(pieces))
    _run_skewed([group_stages(first) for first in range(0, MIX_BATCH, MIX_GROUP)], MIX_LAG,
                lambda anchor: epilogue_piece(todo.pop(0), anchor) if todo else None)
    for k in todo:
        epilogue_piece(k)
    for first, y in pre_norm:
        y_keep[first * ts:(first + MIX_GROUP) * ts, :] = y


def _mixer_ln(h3d, win_t, convw, prm, dng, dnproj, poolw, pscale, pproj, wout, lng, lnb):
    b, s, _ = h3d.shape
    assert win_t.shape == (W_GATE0 + 2 * D_MODEL, D_MODEL)
    operands = (win_t, convw, prm, dng, dnproj, poolw, pscale, pproj, wout, lng, lnb, *_mixer_masks())
    block = (MIX_BATCH, MIX_ROWS, D_MODEL)
    tiles = s // MIX_ROWS
    steps = (b // MIX_BATCH) * tiles

    def block_of(step):
        return step // tiles, step % tiles, 0

    return pl.pallas_call(
        functools.partial(_mixer_kernel, tiles=tiles),
        grid=(steps + 1,),
        in_specs=[pl.BlockSpec(block, lambda k: block_of(jnp.minimum(k, steps - 1)))]
        + [_resident(op.shape) for op in operands],
        out_specs=pl.BlockSpec(block, lambda k: block_of(jnp.maximum(k - 1, 0))),
        out_shape=jax.ShapeDtypeStruct(h3d.shape, F32),
        scratch_shapes=[
            pltpu.VMEM((MIX_BATCH, DN_HEADS, DN_HEAD_DIM, DN_HEAD_DIM), F32),
            pltpu.VMEM((MIX_BATCH, SUBLANES, QKV_COLS), F32),
            pltpu.VMEM((MIX_BATCH, MAX_WINDOW, POOL_WIDTH), F32),
            pltpu.VMEM((MIX_BATCH * MIX_ROWS, D_MODEL), F32),
        ],
        compiler_params=pltpu.CompilerParams(
            dimension_semantics=("arbitrary",), vmem_limit_bytes=MIX_VMEM_LIMIT),
        name="mixer_ln",
    )(h3d, *operands)


def _row(v):
    return v.reshape(1, -1).astype(F32)


def kernel(x, ffn_pre_w_gate, ffn_pre_w_up, ffn_pre_w_down, norm_pre_g, norm_pre_b, mix_w_in, mix_conv_w, dn_a_log, dn_dt_bias, dn_norm_g, dn_w_proj, pool_w, pool_scale, pool_w_proj, mix_w_out, norm_mix_g, norm_mix_b, ffn_post_w_gate, ffn_post_w_up, ffn_post_w_down, norm_post_g, norm_post_b):
    bsz, seq, _ = x.shape
    h = x
    for l in range(DEPTH):
        h = _ffn_ln(h.reshape(bsz * seq, D_MODEL),
                    ffn_pre_w_gate[l], ffn_pre_w_up[l], ffn_pre_w_down[l], _row(norm_pre_g[l]), _row(norm_pre_b[l]),
                    "ffn_ln_pre")

        lanes = MIX_GROUP * MIX_ROWS
        prm = jnp.zeros((2, SUBLANES, lanes), F32)
        prm = prm.at[0, DN_HEADS:2 * DN_HEADS].set(jnp.broadcast_to(dn_a_log[l][:, None], (DN_HEADS, lanes)))
        prm = prm.at[1, DN_HEADS:2 * DN_HEADS].set(jnp.broadcast_to(dn_dt_bias[l][:, None], (DN_HEADS, lanes)))
        h = _mixer_ln(
            h.reshape(bsz, seq, D_MODEL),
            jnp.swapaxes(mix_w_in[l], 0, 1),
            mix_conv_w[l].astype(F32), prm, _row(jnp.tile(dn_norm_g[l], DN_HEADS)),
            dn_w_proj[l], pool_w[l], _row(pool_scale[l]), pool_w_proj[l], mix_w_out[l],
            _row(norm_mix_g[l]), _row(norm_mix_b[l]))

        h = _ffn_ln(h.reshape(bsz * seq, D_MODEL),
                    ffn_post_w_gate[l], ffn_post_w_up[l], ffn_post_w_down[l], _row(norm_post_g[l]), _row(norm_post_b[l]),
                    "ffn_ln_post")
    return h.reshape(bsz, seq, D_MODEL)
```

```python
import numpy as np

import jax
import jax.numpy as jnp
from jax import lax
from jax.experimental import pallas as pl
from jax.experimental.pallas import tpu as pltpu

F32 = jnp.float32
BF16 = jnp.bfloat16

D_MODEL = 1024
D_FF = 2816
DN_HEADS = 4
DN_HEAD_DIM = 128
DN_WIDTH = DN_HEADS * DN_HEAD_DIM
CONV_WIDTH = 4
CHUNK = 64
POOL_WINDOWS = (2, 4, 8, 16)
POOL_GROUPS = 4
POOL_GROUP_DIM = 128
POOL_WIDTH = POOL_GROUPS * POOL_GROUP_DIM
QKV_COLS = 3 * DN_WIDTH
W_Z0 = QKV_COLS
W_BA0 = W_Z0 + DN_WIDTH
W_P0 = W_BA0 + 2 * DN_HEADS
W_GATE0 = W_P0 + POOL_WIDTH
DEPTH = 1
ALPHA = (2.0 * DEPTH) ** 0.25
LN_EPS = 1e-5
RMS_EPS = 1e-6

SUBLANES = 8
LANES = 128
MIB = 1024 * 1024

FFN_ROWS = 512
FFN_EPILOGUE_ROWS = 64
FFN_COLS = 256
MIX_ROWS = 256
MIX_CHUNKS = MIX_ROWS // CHUNK
MIX_BATCH = 2
MAX_WINDOW = max(POOL_WINDOWS)
FFN_VMEM_LIMIT = 58 * MIB
MIX_VMEM_LIMIT = 56 * MIB


def _dot(a, b, **kw):
    return jnp.dot(a, b, preferred_element_type=F32, **kw)


def _dot_nt(a, b):
    return lax.dot_general(a, b, (((1,), (1,)), ((), ())), preferred_element_type=F32)


def _layer_norm(y, g, b):
    mu = jnp.mean(y, axis=-1, keepdims=True)
    yc = y - mu
    var = jnp.mean(yc * yc, axis=-1, keepdims=True)
    return yc * lax.rsqrt(var + LN_EPS) * g + b


def _sigmoid(x):
    return 0.5 * jnp.tanh(0.5 * x) + 0.5


def _silu(x):
    half = 0.5 * x
    return half + half * jnp.tanh(half)


def _head_sums(sq, ones2):
    sq = sq.astype(BF16)
    width = ones2.shape[0]
    return jnp.concatenate(
        [_dot(sq[:, i * width:(i + 1) * width], ones2) for i in range(sq.shape[1] // width)], axis=1)


def _issued_after(value, anchor):
    never = (anchor == anchor) & (anchor != anchor)
    return value + jnp.where(never, 1.0, 0.0)


def _ffn_ln_kernel(x_ref, x_prev_ref, wg_hbm, wu_hbm, wd_hbm, g_ref, b_ref, o_ref,
                   acc_keep, wg_ref, wu_ref, wd_ref, w_sem):
    i = pl.program_id(0)
    last = pl.num_programs(0) - 1
    pieces = FFN_ROWS // FFN_EPILOGUE_ROWS
    n_chunks = D_FF // FFN_COLS
    assert pieces + 2 <= n_chunks

    def weight_copies(c):
        span = pl.ds(c * FFN_COLS, FFN_COLS)
        return (pltpu.make_async_copy(wg_hbm.at[:, span], wg_ref.at[:, span], w_sem.at[0, c]),
                pltpu.make_async_copy(wu_hbm.at[:, span], wu_ref.at[:, span], w_sem.at[1, c]),
                pltpu.make_async_copy(wd_hbm.at[span, :], wd_ref.at[span, :], w_sem.at[2, c]))

    def epilogue_piece(k, anchor=None):
        rows = slice(k * FFN_EPILOGUE_ROWS, (k + 1) * FFN_EPILOGUE_ROWS)
        x_rows = x_prev_ref[rows, :]
        if anchor is not None:
            x_rows = _issued_after(x_rows, anchor)
        y = ALPHA * x_rows + 0.5 * acc_keep[rows, :]
        o_ref[rows, :] = _layer_norm(y, g_ref[...], b_ref[...])

    def tile_matmuls(first_step):
        x = x_ref[...]
        xb = x.astype(BF16)
        acc = jnp.zeros(x.shape, F32)
        for c in range(n_chunks):
            if first_step:
                for copy in weight_copies(c):
                    copy.wait()
            cols = slice(c * FFN_COLS, (c + 1) * FFN_COLS)
            gate = _dot(xb, wg_ref[:, cols].astype(BF16))
            up = _dot(xb, wu_ref[:, cols].astype(BF16))
            acc = acc + _dot((_silu(gate) * up).astype(BF16), wd_ref[cols, :].astype(BF16))
            if 2 <= c < pieces + 2:
                epilogue_piece(c - 2, acc[0:1, :])
        acc_keep[...] = acc

    @pl.when(i == 0)
    def _():
        acc_keep[...] = jnp.zeros(acc_keep.shape, F32)
        for c in range(n_chunks):
            for copy in weight_copies(c):
                copy.start()
        tile_matmuls(first_step=True)

    @pl.when((i > 0) & (i < last))
    def _():
        tile_matmuls(first_step=False)

    @pl.when(i == last)
    def _():
        for k in range(pieces):
            epilogue_piece(k)


def _resident(shape):
    return pl.BlockSpec(shape, lambda *_: (0,) * len(shape), pipeline_mode=pl.Buffered(1))


def _ffn_ln(x2d, wg, wu, wd, g, b, name):
    t = x2d.shape[0]
    tiles = t // FFN_ROWS
    return pl.pallas_call(
        _ffn_ln_kernel,
        grid=(tiles + 1,),
        in_specs=[
            pl.BlockSpec((FFN_ROWS, D_MODEL), lambda i: (jnp.minimum(i, tiles - 1), 0)),
            pl.BlockSpec((FFN_ROWS, D_MODEL), lambda i: (jnp.maximum(i - 1, 0), 0)),
            pl.BlockSpec(memory_space=pl.ANY),
            pl.BlockSpec(memory_space=pl.ANY),
            pl.BlockSpec(memory_space=pl.ANY),
            _resident((1, D_MODEL)),
            _resident((1, D_MODEL)),
        ],
        out_specs=pl.BlockSpec((FFN_ROWS, D_MODEL), lambda i: (jnp.maximum(i - 1, 0), 0)),
        out_shape=jax.ShapeDtypeStruct((t, D_MODEL), F32),
        scratch_shapes=[
            pltpu.VMEM((FFN_ROWS, D_MODEL), F32),
            pltpu.VMEM(wg.shape, F32),
            pltpu.VMEM(wu.shape, F32),
            pltpu.VMEM(wd.shape, F32),
            pltpu.SemaphoreType.DMA((3, D_FF // FFN_COLS)),
        ],
        compiler_params=pltpu.CompilerParams(
            dimension_semantics=("arbitrary",), vmem_limit_bytes=FFN_VMEM_LIMIT),
        name=name,
    )(x2d, x2d, wg, wu, wd, g, b)


def _mixer_masks():
    r = np.arange(MIX_ROWS)[:, None]
    c = np.arange(MIX_ROWS)[None, :]
    block_diag = (r // CHUNK == c // CHUNK)
    in_win = np.stack([(c <= r) & (c > r - w) for w in POOL_WINDOWS])
    rt = np.arange(MAX_WINDOW)[:, None]
    ct = np.arange(MAX_WINDOW)[None, :]
    in_tail = np.stack([(ct - MAX_WINDOW > rt - w) for w in POOL_WINDOWS])
    head_pair_ones = (r // DN_HEAD_DIM == c // DN_HEAD_DIM)
    as_bf16 = lambda m: jnp.asarray(m.astype(np.float32), dtype=BF16)
    return as_bf16(block_diag), as_bf16(in_win), as_bf16(in_tail), as_bf16(head_pair_ones)


def _mixer_kernel(h_ref, win_ref, convw_ref, prm_ref, dng_ref,
                  dnproj_ref, poolw_ref, pscale_ref, pproj_ref, wout_ref, lng_ref, lnb_ref,
                  bdmask_ref, inwin_ref, intail_ref, ones2_ref,
                  o_ref, state_ref, xtail_ref, ptail_ref):
    ts = MIX_ROWS
    nb = MIX_BATCH
    tile = pl.program_id(1)
    seqs = range(nb)
    rows_of = lambda s: slice(s * ts, (s + 1) * ts)

    @pl.when(tile == 0)
    def _():
        state_ref[...] = jnp.zeros(state_ref.shape, F32)
        xtail_ref[...] = jnp.zeros(xtail_ref.shape, F32)
        ptail_ref[...] = jnp.zeros(ptail_ref.shape, F32)

    ones2 = ones2_ref[...]
    bdmask = bdmask_ref[...]
    prow = lax.broadcasted_iota(jnp.int32, (CHUNK, ts), 0)
    plane = lax.broadcasted_iota(jnp.int32, (CHUNK, ts), 1)
    pcol = plane % CHUNK
    in_block = [plane // CHUNK == c for c in range(MIX_CHUNKS)]
    causal_p = prow >= pcol
    strict_p = prow > pcol
    eye_p = (prow == pcol).astype(F32)
    low_lanes = lax.broadcasted_iota(jnp.int32, (CHUNK, LANES), 1) < CHUNK
    t_abs = tile * ts + lax.broadcasted_iota(jnp.int32, (ts, POOL_GROUP_DIM), 0)

    def column_broadcast(row):
        return jnp.broadcast_to(row, (DN_HEAD_DIM, ts)).T

    def pack(full):
        out = full[0:CHUNK]
        for c in range(1, MIX_CHUNKS):
            out = jnp.where(in_block[c], full[c * CHUNK:(c + 1) * CHUNK], out)
        return out

    def block_diag(packed):
        return jnp.concatenate([packed.astype(BF16)] * MIX_CHUNKS, axis=0) * bdmask

    h = h_ref[...].reshape(nb * ts, D_MODEL)
    hb = h.astype(BF16)

    def project(first_col, width):
        return _dot_nt(hb, win_ref[first_col:first_col + width, :].astype(BF16))

    def write_strength_and_decay():
        ba_t = _dot_nt(win_ref[W_BA0:W_BA0 + SUBLANES, :].astype(BF16), hb)
        beta = _sigmoid(ba_t)
        a_shift = ba_t + prm_ref[1]
        softplus = jnp.maximum(a_shift, 0.0) + jnp.log1p(jnp.exp(-jnp.abs(a_shift)))
        g = -jnp.exp(prm_ref[0]) * softplus
        lane_in_chunk = lax.broadcasted_iota(jnp.int32, (SUBLANES, nb * ts), 1) % CHUNK
        shift = 1
        while shift < CHUNK:
            rolled = jnp.concatenate(
                [pltpu.roll(g[:, i * LANES:(i + 1) * LANES], shift, 1) for i in range(nb * ts // LANES)],
                axis=1)
            g = g + jnp.where(lane_in_chunk >= shift, rolled, 0.0)
            shift *= 2
        return beta, g

    def gate_group(j):
        return _sigmoid(project(W_GATE0 + j * DN_WIDTH, DN_WIDTH))

    qkv = []
    gate_sig = []
    for part in range(3):
        cols = slice(part * DN_WIDTH, (part + 1) * DN_WIDTH)
        raw = project(part * DN_WIDTH, DN_WIDTH)
        cw = convw_ref[:, cols]
        conv = []
        for s in seqs:
            raw_s = raw[rows_of(s)]
            ext = jnp.concatenate([xtail_ref[s, :, cols], raw_s], axis=0)
            xtail_ref[s, :, cols] = raw_s[ts - SUBLANES:, :]
            y = pltpu.roll(ext, CONV_WIDTH - 1, 0)[SUBLANES:] * cw[0:1]
            for tap in range(1, CONV_WIDTH - 1):
                y = y + pltpu.roll(ext, CONV_WIDTH - 1 - tap, 0)[SUBLANES:] * cw[tap:tap + 1]
            conv.append(y + raw_s * cw[CONV_WIDTH - 1:CONV_WIDTH])
        qkv.append(_silu(jnp.concatenate(conv, axis=0)))
        gate_sig.append(gate_group(part))
        if part == 1:
            beta_t, gr = write_strength_and_decay()
    q_all = qkv[0] * (lax.rsqrt(_head_sums(qkv[0] * qkv[0], ones2) + RMS_EPS) * (DN_HEAD_DIM ** -0.5))
    k_all = qkv[1] * lax.rsqrt(_head_sums(qkv[1] * qkv[1], ones2) + RMS_EPS)
    v_all = qkv[2]

    p = project(W_P0, POOL_WIDTH)
    pooled = []
    for s in seqs:
        p_s = p[rows_of(s)]
        tail = ptail_ref[s]
        ptail_ref[s] = p_s[ts - MAX_WINDOW:, :]
        pooled_s = []
        for gi, win in enumerate(POOL_WINDOWS):
            lo = gi * POOL_GROUP_DIM
            pg = p_s[:, lo:lo + POOL_GROUP_DIM]
            wsum = _dot(inwin_ref[gi], pg.astype(BF16))
            top = wsum[:MAX_WINDOW] + _dot(intail_ref[gi], tail[:, lo:lo + POOL_GROUP_DIM].astype(BF16))
            wsum = jnp.concatenate([top, wsum[MAX_WINDOW:]], axis=0)
            count = jnp.minimum(t_abs + 1, win).astype(F32)
            pooled_s.append(wsum / count - pg)
        pooled.append(jnp.concatenate(pooled_s, axis=1))
    pooled = jnp.concatenate(pooled, axis=0).astype(BF16)
    pooled = jnp.concatenate(
        [_dot(pooled[:, gi * POOL_GROUP_DIM:(gi + 1) * POOL_GROUP_DIM], poolw_ref[gi].astype(BF16))
         for gi in range(POOL_GROUPS)], axis=1) * pscale_ref[...]
    y_pool = _dot(pooled.astype(BF16), pproj_ref[...].astype(BF16))
    z = project(W_Z0, DN_WIDTH)
    gate_sig.append(gate_group(3))

    chains = [(s, hd) for s in seqs for hd in range(DN_HEADS)]
    ids = range(len(chains))
    qn, kn_b, k_beta, v_beta, eg_b, kd_t, decay_p = [], [], [], [], [], [], []
    for s, hd in chains:
        lanes = slice(hd * DN_HEAD_DIM, (hd + 1) * DN_HEAD_DIM)
        qn.append(q_all[rows_of(s), lanes])
        kn = k_all[rows_of(s), lanes]
        v = v_all[rows_of(s), lanes]
        beta_b = column_broadcast(beta_t[hd:hd + 1, rows_of(s)])
        gr_row = gr[DN_HEADS + hd:DN_HEADS + hd + 1, rows_of(s)]
        gc_b = column_broadcast(gr_row)
        gc_p = jnp.concatenate(
            [jnp.where(low_lanes, gc_b[2 * i * CHUNK:(2 * i + 1) * CHUNK],
                       gc_b[(2 * i + 1) * CHUNK:(2 * i + 2) * CHUNK])
             for i in range(MIX_CHUNKS // 2)], axis=1)
        decay_p.append(jnp.exp(jnp.where(causal_p, gc_p - gr_row, -jnp.inf)))
        eg_b.append(jnp.exp(gc_b))
        gl_b = jnp.concatenate(
            [jnp.broadcast_to(gc_b[c * CHUNK + CHUNK - 1:(c + 1) * CHUNK, :], (CHUNK, DN_HEAD_DIM))
             for c in range(MIX_CHUNKS)], axis=0)
        k_beta.append(kn * beta_b)
        v_beta.append(v * beta_b)
        kn_b.append(kn.astype(BF16))
        kd_t.append((kn * jnp.exp(gl_b - gc_b)).T.astype(BF16))
    gram = [_dot_nt(jnp.concatenate([k_beta[i], qn[i]], axis=0).astype(BF16), kn_b[i]) for i in ids]
    lower = [pack(gram[i][:ts]) * jnp.where(strict_p, decay_p[i], 0.0) for i in ids]
    qk = [block_diag(pack(gram[i][ts:]) * decay_p[i]) for i in ids]
    t_mat = [eye_p - lower[i] for i in ids]
    power = [_dot(lower[i].astype(BF16), block_diag(lower[i])) for i in ids]
    for _ in range(4):
        both = [_dot(jnp.concatenate([power[i], t_mat[i]], axis=0).astype(BF16), block_diag(power[i]))
                for i in ids]
        power = [both[i][:CHUNK] for i in ids]
        t_mat = [t_mat[i] + both[i][CHUNK:] for i in ids]
    t_mat = [t_mat[i] + _dot(t_mat[i].astype(BF16), block_diag(power[i])) for i in ids]
    uw = [_dot(block_diag(t_mat[i]),
               jnp.concatenate([v_beta[i], k_beta[i] * eg_b[i]], axis=1).astype(BF16)) for i in ids]
    q_dec = [qn[i] * eg_b[i] for i in ids]
    state = [state_ref[s, hd] for s, hd in chains]
    kd_uw = []
    for i in ids:
        per_chunk = []
        for c in range(MIX_CHUNKS):
            uw_pad = jnp.concatenate(
                [uw[i][cc * CHUNK:(cc + 1) * CHUNK] if cc == c else jnp.zeros((CHUNK, 2 * DN_HEAD_DIM), F32)
                 for cc in range(MIX_CHUNKS)], axis=0).astype(BF16)
            per_chunk.append(_dot(kd_t[i], uw_pad))
        kd_uw.append(per_chunk)
    o_inter = [[] for _ in ids]
    v_new_all = [[] for _ in ids]
    for c in range(MIX_CHUNKS):
        r0 = c * CHUNK
        for i in ids:
            u_c = uw[i][r0:r0 + CHUNK, :DN_HEAD_DIM]
            w_c = uw[i][r0:r0 + CHUNK, DN_HEAD_DIM:]
            lhs = jnp.concatenate(
                [kd_uw[i][c][:, DN_HEAD_DIM:], w_c, q_dec[i][r0:r0 + CHUNK]], axis=0).astype(BF16)
            prod = _dot(lhs, state[i].astype(BF16))
            v_new_all[i].append(u_c - prod[DN_HEAD_DIM:DN_HEAD_DIM + CHUNK])
            o_inter[i].append(prod[DN_HEAD_DIM + CHUNK:])
            g_last = eg_b[i][r0 + CHUNK - 1:r0 + CHUNK, :]
            state[i] = state[i] * g_last - prod[:DN_HEAD_DIM] + kd_uw[i][c][:, :DN_HEAD_DIM]
    o_chain = []
    for i, (s, hd) in enumerate(chains):
        state_ref[s, hd] = state[i]
        v_new_full = jnp.concatenate(v_new_all[i], axis=0).astype(BF16)
        o_chain.append(jnp.concatenate(o_inter[i], axis=0) + _dot(qk[i], v_new_full))
    o_all = jnp.concatenate(
        [jnp.concatenate(o_chain[s * DN_HEADS:(s + 1) * DN_HEADS], axis=1) for s in seqs], axis=0)
    o_all = o_all * lax.rsqrt(_head_sums(o_all * o_all, ones2) * (1.0 / DN_HEAD_DIM) + RMS_EPS)
    o_all = o_all * dng_ref[...] * _silu(z)
    y_dn = _dot(o_all.astype(BF16), dnproj_ref[...].astype(BF16))

    merged = (jnp.concatenate(gate_sig[:2], axis=1) * y_dn
              + jnp.concatenate(gate_sig[2:], axis=1) * y_pool)
    m = _dot(merged.astype(BF16), wout_ref[...].astype(BF16))
    o_ref[...] = _layer_norm(ALPHA * h + m, lng_ref[...], lnb_ref[...]).reshape(nb, ts, D_MODEL)


def _mixer_ln(h3d, win_t, convw, prm, dng, dnproj, poolw, pscale, pproj, wout, lng, lnb):
    b, s, _ = h3d.shape
    assert win_t.shape == (W_GATE0 + 2 * D_MODEL, D_MODEL) and 2 * DN_HEADS == SUBLANES
    operands = (win_t, convw, prm, dng, dnproj, poolw, pscale, pproj, wout, lng, lnb, *_mixer_masks())
    block = (MIX_BATCH, MIX_ROWS, D_MODEL)
    return pl.pallas_call(
        _mixer_kernel,
        grid=(b // MIX_BATCH, s // MIX_ROWS),
        in_specs=[pl.BlockSpec(block, lambda i, j: (i, j, 0))] + [_resident(op.shape) for op in operands],
        out_specs=pl.BlockSpec(block, lambda i, j: (i, j, 0)),
        out_shape=jax.ShapeDtypeStruct(h3d.shape, F32),
        scratch_shapes=[
            pltpu.VMEM((MIX_BATCH, DN_HEADS, DN_HEAD_DIM, DN_HEAD_DIM), F32),
            pltpu.VMEM((MIX_BATCH, SUBLANES, QKV_COLS), F32),
            pltpu.VMEM((MIX_BATCH, MAX_WINDOW, POOL_WIDTH), F32),
        ],
        compiler_params=pltpu.CompilerParams(
            dimension_semantics=("arbitrary", "arbitrary"), vmem_limit_bytes=MIX_VMEM_LIMIT),
        name="mixer_ln",
    )(h3d, *operands)


def _row(v):
    return v.reshape(1, -1).astype(F32)


def kernel(x, ffn_pre_w_gate, ffn_pre_w_up, ffn_pre_w_down, norm_pre_g, norm_pre_b, mix_w_in, mix_conv_w, dn_a_log, dn_dt_bias, dn_norm_g, dn_w_proj, pool_w, pool_scale, pool_w_proj, mix_w_out, norm_mix_g, norm_mix_b, ffn_post_w_gate, ffn_post_w_up, ffn_post_w_down, norm_post_g, norm_post_b):
    bsz, seq, _ = x.shape
    h = x
    for l in range(DEPTH):
        h = _ffn_ln(h.reshape(bsz * seq, D_MODEL),
                    ffn_pre_w_gate[l], ffn_pre_w_up[l], ffn_pre_w_down[l], _row(norm_pre_g[l]), _row(norm_pre_b[l]),
                    "ffn_ln_pre")

        lanes = MIX_BATCH * MIX_ROWS
        prm = jnp.zeros((2, SUBLANES, lanes), F32)
        prm = prm.at[0, DN_HEADS:2 * DN_HEADS].set(jnp.broadcast_to(dn_a_log[l][:, None], (DN_HEADS, lanes)))
        prm = prm.at[1, DN_HEADS:2 * DN_HEADS].set(jnp.broadcast_to(dn_dt_bias[l][:, None], (DN_HEADS, lanes)))
        h = _mixer_ln(
            h.reshape(bsz, seq, D_MODEL),
            jnp.swapaxes(mix_w_in[l], 0, 1),
            mix_conv_w[l].astype(F32), prm, _row(jnp.tile(dn_norm_g[l], DN_HEADS)),
            dn_w_proj[l], pool_w[l], _row(pool_scale[l]), pool_w_proj[l], mix_w_out[l],
            _row(norm_mix_g[l]), _row(norm_mix_b[l]))

        h = _ffn_ln(h.reshape(bsz * seq, D_MODEL),
                    ffn_post_w_gate[l], ffn_post_w_up[l], ffn_post_w_down[l], _row(norm_post_g[l]), _row(norm_post_b[l]),
                    "ffn_ln_post")
    return h.reshape(bsz, seq, D_MODEL)
```

```python
import numpy as np

import jax
import jax.numpy as jnp
from jax import lax
from jax.experimental import pallas as pl
from jax.experimental.pallas import tpu as pltpu

F32 = jnp.float32
BF16 = jnp.bfloat16

D_MODEL = 1024
D_FF = 2816
DN_HEADS = 4
DN_HEAD_DIM = 128
DN_WIDTH = DN_HEADS * DN_HEAD_DIM
CONV_WIDTH = 4
CHUNK = 64
POOL_WINDOWS = (2, 4, 8, 16)
POOL_GROUPS = 4
POOL_GROUP_DIM = 128
POOL_WIDTH = POOL_GROUPS * POOL_GROUP_DIM
QKV_COLS = 3 * DN_WIDTH
W_Z0 = QKV_COLS
W_BA0 = W_Z0 + DN_WIDTH
W_P0 = W_BA0 + 2 * DN_HEADS
W_GATE0 = W_P0 + POOL_WIDTH
DEPTH = 1
ALPHA = (2.0 * DEPTH) ** 0.25
LN_EPS = 1e-5
RMS_EPS = 1e-6

SUBLANES = 8
LANES = 128
MIB = 1024 * 1024

FFN_ROWS = 512
FFN_EPILOGUE_ROWS = 64
FFN_COLS = 256
MIX_ROWS = 256
MIX_CHUNKS = MIX_ROWS // CHUNK
MIX_BATCH = 2
MAX_WINDOW = max(POOL_WINDOWS)
PREP_ROWS = 512
SLAB_COLS = 512
W_PIECES = (0, DN_WIDTH, 2 * DN_WIDTH, W_Z0, W_P0) + tuple(W_GATE0 + j * DN_WIDTH for j in range(4))
FFN_VMEM_LIMIT = 58 * MIB
MIX_VMEM_LIMIT = 56 * MIB


def _dot(a, b, **kw):
    return jnp.dot(a, b, preferred_element_type=F32, **kw)


def _dot_nt(a, b):
    return lax.dot_general(a, b, (((1,), (1,)), ((), ())), preferred_element_type=F32)


def _slab_dot(a, w_slabs):
    return jnp.concatenate([_dot(a, w_slabs[j]) for j in range(w_slabs.shape[0])], axis=1)


def _layer_norm(y, g, b):
    mu = jnp.mean(y, axis=-1, keepdims=True)
    yc = y - mu
    var = jnp.mean(yc * yc, axis=-1, keepdims=True)
    return yc * lax.rsqrt(var + LN_EPS) * g + b


def _sigmoid(x):
    return 0.5 * jnp.tanh(0.5 * x) + 0.5


def _silu(x):
    half = 0.5 * x
    return half + half * jnp.tanh(half)


def _head_sums(sq, ones2):
    sq = sq.astype(BF16)
    width = ones2.shape[0]
    return jnp.concatenate(
        [_dot(sq[:, i * width:(i + 1) * width], ones2) for i in range(sq.shape[1] // width)], axis=1)


def _issued_after(value, anchor):
    never = (anchor == anchor) & (anchor != anchor)
    return value + jnp.where(never, 1.0, 0.0)


def _ffn_ln_kernel(x_ref, x_prev_ref, wg_hbm, wu_hbm, wd_hbm, g_ref, b_ref, o_ref,
                   acc_keep, wg_ref, wu_ref, wd_ref, w_sem):
    i = pl.program_id(0)
    last = pl.num_programs(0) - 1
    pieces = FFN_ROWS // FFN_EPILOGUE_ROWS
    n_chunks = D_FF // FFN_COLS
    assert pieces + 2 <= n_chunks

    def weight_copies(c):
        span = pl.ds(c * FFN_COLS, FFN_COLS)
        return (pltpu.make_async_copy(wg_hbm.at[:, span], wg_ref.at[:, span], w_sem.at[0, c]),
                pltpu.make_async_copy(wu_hbm.at[:, span], wu_ref.at[:, span], w_sem.at[1, c]),
                pltpu.make_async_copy(wd_hbm.at[span, :], wd_ref.at[span, :], w_sem.at[2, c]))

    def epilogue_piece(k, anchor=None):
        rows = slice(k * FFN_EPILOGUE_ROWS, (k + 1) * FFN_EPILOGUE_ROWS)
        x_rows = x_prev_ref[rows, :]
        if anchor is not None:
            x_rows = _issued_after(x_rows, anchor)
        y = ALPHA * x_rows + 0.5 * acc_keep[rows, :]
        o_ref[rows, :] = _layer_norm(y, g_ref[...], b_ref[...])

    def tile_matmuls(first_step):
        x = x_ref[...]
        xb = x.astype(BF16)
        acc = jnp.zeros(x.shape, F32)
        for c in range(n_chunks):
            if first_step:
                for copy in weight_copies(c):
                    copy.wait()
            cols = slice(c * FFN_COLS, (c + 1) * FFN_COLS)
            gate = _dot(xb, wg_ref[:, cols].astype(BF16))
            up = _dot(xb, wu_ref[:, cols].astype(BF16))
            acc = acc + _dot((_silu(gate) * up).astype(BF16), wd_ref[cols, :].astype(BF16))
            if 2 <= c < pieces + 2:
                epilogue_piece(c - 2, acc[0:1, :])
        acc_keep[...] = acc

    @pl.when(i == 0)
    def _():
        acc_keep[...] = jnp.zeros(acc_keep.shape, F32)
        for c in range(n_chunks):
            for copy in weight_copies(c):
                copy.start()
        tile_matmuls(first_step=True)

    @pl.when((i > 0) & (i < last))
    def _():
        tile_matmuls(first_step=False)

    @pl.when(i == last)
    def _():
        for k in range(pieces):
            epilogue_piece(k)


def _resident(shape):
    return pl.BlockSpec(shape, lambda *_: (0,) * len(shape), pipeline_mode=pl.Buffered(1))


def _ffn_ln(x2d, wg, wu, wd, g, b, name):
    t = x2d.shape[0]
    tiles = t // FFN_ROWS
    return pl.pallas_call(
        _ffn_ln_kernel,
        grid=(tiles + 1,),
        in_specs=[
            pl.BlockSpec((FFN_ROWS, D_MODEL), lambda i: (jnp.minimum(i, tiles - 1), 0)),
            pl.BlockSpec((FFN_ROWS, D_MODEL), lambda i: (jnp.maximum(i - 1, 0), 0)),
            pl.BlockSpec(memory_space=pl.ANY),
            pl.BlockSpec(memory_space=pl.ANY),
            pl.BlockSpec(memory_space=pl.ANY),
            _resident((1, D_MODEL)),
            _resident((1, D_MODEL)),
        ],
        out_specs=pl.BlockSpec((FFN_ROWS, D_MODEL), lambda i: (jnp.maximum(i - 1, 0), 0)),
        out_shape=jax.ShapeDtypeStruct((t, D_MODEL), F32),
        scratch_shapes=[
            pltpu.VMEM((FFN_ROWS, D_MODEL), F32),
            pltpu.VMEM(wg.shape, F32),
            pltpu.VMEM(wu.shape, F32),
            pltpu.VMEM(wd.shape, F32),
            pltpu.SemaphoreType.DMA((3, D_FF // FFN_COLS)),
        ],
        compiler_params=pltpu.CompilerParams(
            dimension_semantics=("arbitrary",), vmem_limit_bytes=FFN_VMEM_LIMIT),
        name=name,
    )(x2d, x2d, wg, wu, wd, g, b)


def _mixer_masks():
    r = np.arange(MIX_ROWS)[:, None]
    c = np.arange(MIX_ROWS)[None, :]
    block_diag = (r // CHUNK == c // CHUNK)
    in_win = np.stack([(c <= r) & (c > r - w) for w in POOL_WINDOWS])
    rt = np.arange(MAX_WINDOW)[:, None]
    ct = np.arange(MAX_WINDOW)[None, :]
    in_tail = np.stack([(ct - MAX_WINDOW > rt - w) for w in POOL_WINDOWS])
    head_pair_ones = (r // DN_HEAD_DIM == c // DN_HEAD_DIM)
    as_bf16 = lambda m: jnp.asarray(m.astype(np.float32), dtype=BF16)
    return as_bf16(block_diag), as_bf16(in_win), as_bf16(in_tail), as_bf16(head_pair_ones)


def _mixer_kernel(h_ref, win_hbm, convw_ref, prm_ref, dng_ref,
                  dnproj_hbm, poolw_ref, pscale_ref, pproj_hbm, wout_hbm, lng_ref, lnb_ref,
                  bdmask_ref, inwin_ref, intail_ref, ones2_ref,
                  o_ref, state_ref, xtail_ref, ptail_ref,
                  win_bf, wba_ref, dnproj_bf, pproj_bf, wout_bf, stage, prep_sem):
    ts = MIX_ROWS
    nb = MIX_BATCH
    tile = pl.program_id(1)
    seqs = range(nb)
    rows_of = lambda s: slice(s * ts, (s + 1) * ts)

    @pl.when((pl.program_id(0) == 0) & (tile == 0))
    def _():
        def store_transposed(k):
            def store(x):
                win_bf[k] = x.T.astype(BF16)
            return store

        def store_slabs(dst, row_block):
            def store(x):
                rows = slice(row_block * PREP_ROWS, (row_block + 1) * PREP_ROWS)
                for j in range(D_MODEL // SLAB_COLS):
                    dst[j, rows, :] = x[:, j * SLAB_COLS:(j + 1) * SLAB_COLS].astype(BF16)
            return store

        jobs = [(win_hbm.at[pl.ds(first, PREP_ROWS), :], store_transposed(k))
                for k, first in enumerate(W_PIECES)]
        jobs.append((dnproj_hbm, store_slabs(dnproj_bf, 0)))
        jobs.append((pproj_hbm, store_slabs(pproj_bf, 0)))
        jobs += [(wout_hbm.at[pl.ds(r * PREP_ROWS, PREP_ROWS), :], store_slabs(wout_bf, r))
                 for r in range(D_MODEL // PREP_ROWS)]
        ba_copy = pltpu.make_async_copy(win_hbm.at[pl.ds(W_BA0, SUBLANES), :], wba_ref, prep_sem.at[2])
        ba_copy.start()
        copies = [pltpu.make_async_copy(src, stage.at[n % 2], prep_sem.at[n % 2])
                  for n, (src, _) in enumerate(jobs)]
        copies[0].start()
        for n, (_, store) in enumerate(jobs):
            if n + 1 < len(jobs):
                copies[n + 1].start()
            copies[n].wait()
            store(stage[n % 2])
        ba_copy.wait()

    @pl.when(tile == 0)
    def _():
        state_ref[...] = jnp.zeros(state_ref.shape, F32)
        xtail_ref[...] = jnp.zeros(xtail_ref.shape, F32)
        ptail_ref[...] = jnp.zeros(ptail_ref.shape, F32)

    ones2 = ones2_ref[...]
    bdmask = bdmask_ref[...]
    prow = lax.broadcasted_iota(jnp.int32, (CHUNK, ts), 0)
    plane = lax.broadcasted_iota(jnp.int32, (CHUNK, ts), 1)
    pcol = plane % CHUNK
    in_block = [plane // CHUNK == c for c in range(MIX_CHUNKS)]
    causal_p = prow >= pcol
    strict_p = prow > pcol
    eye_p = (prow == pcol).astype(F32)
    low_lanes = lax.broadcasted_iota(jnp.int32, (CHUNK, LANES), 1) < CHUNK
    t_abs = tile * ts + lax.broadcasted_iota(jnp.int32, (ts, POOL_GROUP_DIM), 0)

    def column_broadcast(row):
        return jnp.broadcast_to(row, (DN_HEAD_DIM, ts)).T

    def pack(full):
        out = full[0:CHUNK]
        for c in range(1, MIX_CHUNKS):
            out = jnp.where(in_block[c], full[c * CHUNK:(c + 1) * CHUNK], out)
        return out

    def block_diag(packed):
        return jnp.concatenate([packed.astype(BF16)] * MIX_CHUNKS, axis=0) * bdmask

    h = h_ref[...].reshape(nb * ts, D_MODEL)
    hb = h.astype(BF16)

    def project(first_col, width):
        assert width == PREP_ROWS
        return _dot(hb, win_bf[W_PIECES.index(first_col)])

    def write_strength_and_decay():
        ba_t = _dot_nt(wba_ref[...].astype(BF16), hb)
        beta = _sigmoid(ba_t)
        a_shift = ba_t + prm_ref[1]
        softplus = jnp.maximum(a_shift, 0.0) + jnp.log1p(jnp.exp(-jnp.abs(a_shift)))
        g = -jnp.exp(prm_ref[0]) * softplus
        lane_in_chunk = lax.broadcasted_iota(jnp.int32, (SUBLANES, nb * ts), 1) % CHUNK
        shift = 1
        while shift < CHUNK:
            rolled = jnp.concatenate(
                [pltpu.roll(g[:, i * LANES:(i + 1) * LANES], shift, 1) for i in range(nb * ts // LANES)],
                axis=1)
            g = g + jnp.where(lane_in_chunk >= shift, rolled, 0.0)
            shift *= 2
        return beta, g

    def gate_group(j):
        return _sigmoid(project(W_GATE0 + j * DN_WIDTH, DN_WIDTH))

    qkv = []
    gate_sig = []
    for part in range(3):
        cols = slice(part * DN_WIDTH, (part + 1) * DN_WIDTH)
        raw = project(part * DN_WIDTH, DN_WIDTH)
        cw = convw_ref[:, cols]
        conv = []
        for s in seqs:
            raw_s = raw[rows_of(s)]
            ext = jnp.concatenate([xtail_ref[s, :, cols], raw_s], axis=0)
            xtail_ref[s, :, cols] = raw_s[ts - SUBLANES:, :]
            y = pltpu.roll(ext, CONV_WIDTH - 1, 0)[SUBLANES:] * cw[0:1]
            for tap in range(1, CONV_WIDTH - 1):
                y = y + pltpu.roll(ext, CONV_WIDTH - 1 - tap, 0)[SUBLANES:] * cw[tap:tap + 1]
            conv.append(y + raw_s * cw[CONV_WIDTH - 1:CONV_WIDTH])
        qkv.append(_silu(jnp.concatenate(conv, axis=0)))
        gate_sig.append(gate_group(part))
        if part == 1:
            beta_t, gr = write_strength_and_decay()
    q_all = qkv[0] * (lax.rsqrt(_head_sums(qkv[0] * qkv[0], ones2) + RMS_EPS) * (DN_HEAD_DIM ** -0.5))
    k_all = qkv[1] * lax.rsqrt(_head_sums(qkv[1] * qkv[1], ones2) + RMS_EPS)
    v_all = qkv[2]

    p = project(W_P0, POOL_WIDTH)
    pooled = []
    for s in seqs:
        p_s = p[rows_of(s)]
        tail = ptail_ref[s]
        ptail_ref[s] = p_s[ts - MAX_WINDOW:, :]
        pooled_s = []
        for gi, win in enumerate(POOL_WINDOWS):
            lo = gi * POOL_GROUP_DIM
            pg = p_s[:, lo:lo + POOL_GROUP_DIM]
            wsum = _dot(inwin_ref[gi], pg.astype(BF16))
            top = wsum[:MAX_WINDOW] + _dot(intail_ref[gi], tail[:, lo:lo + POOL_GROUP_DIM].astype(BF16))
            wsum = jnp.concatenate([top, wsum[MAX_WINDOW:]], axis=0)
            count = jnp.minimum(t_abs + 1, win).astype(F32)
            pooled_s.append(wsum / count - pg)
        pooled.append(jnp.concatenate(pooled_s, axis=1))
    pooled = jnp.concatenate(pooled, axis=0).astype(BF16)
    pooled = jnp.concatenate(
        [_dot(pooled[:, gi * POOL_GROUP_DIM:(gi + 1) * POOL_GROUP_DIM], poolw_ref[gi].astype(BF16))
         for gi in range(POOL_GROUPS)], axis=1) * pscale_ref[...]
    y_pool = _slab_dot(pooled.astype(BF16), pproj_bf)
    z = project(W_Z0, DN_WIDTH)
    gate_sig.append(gate_group(3))

    chains = [(s, hd) for s in seqs for hd in range(DN_HEADS)]
    ids = range(len(chains))
    qn, kn_b, k_beta, v_beta, eg_b, kd_t, decay_p = [], [], [], [], [], [], []
    for s, hd in chains:
        lanes = slice(hd * DN_HEAD_DIM, (hd + 1) * DN_HEAD_DIM)
        qn.append(q_all[rows_of(s), lanes])
        kn = k_all[rows_of(s), lanes]
        v = v_all[rows_of(s), lanes]
        beta_b = column_broadcast(beta_t[hd:hd + 1, rows_of(s)])
        gr_row = gr[DN_HEADS + hd:DN_HEADS + hd + 1, rows_of(s)]
        gc_b = column_broadcast(gr_row)
        gc_p = jnp.concatenate(
            [jnp.where(low_lanes, gc_b[2 * i * CHUNK:(2 * i + 1) * CHUNK],
                       gc_b[(2 * i + 1) * CHUNK:(2 * i + 2) * CHUNK])
             for i in range(MIX_CHUNKS // 2)], axis=1)
        decay_p.append(jnp.exp(jnp.where(causal_p, gc_p - gr_row, -jnp.inf)))
        eg_b.append(jnp.exp(gc_b))
        gl_b = jnp.concatenate(
            [jnp.broadcast_to(gc_b[c * CHUNK + CHUNK - 1:(c + 1) * CHUNK, :], (CHUNK, DN_HEAD_DIM))
             for c in range(MIX_CHUNKS)], axis=0)
        k_beta.append(kn * beta_b)
        v_beta.append(v * beta_b)
        kn_b.append(kn.astype(BF16))
        kd_t.append((kn * jnp.exp(gl_b - gc_b)).T.astype(BF16))
    gram = [_dot_nt(jnp.concatenate([k_beta[i], qn[i]], axis=0).astype(BF16), kn_b[i]) for i in ids]
    lower = [pack(gram[i][:ts]) * jnp.where(strict_p, decay_p[i], 0.0) for i in ids]
    qk = [block_diag(pack(gram[i][ts:]) * decay_p[i]) for i in ids]
    t_mat = [eye_p - lower[i] for i in ids]
    power = [_dot(lower[i].astype(BF16), block_diag(lower[i])) for i in ids]
    for _ in range(4):
        both = [_dot(jnp.concatenate([power[i], t_mat[i]], axis=0).astype(BF16), block_diag(power[i]))
                for i in ids]
        power = [both[i][:CHUNK] for i in ids]
        t_mat = [t_mat[i] + both[i][CHUNK:] for i in ids]
    t_mat = [t_mat[i] + _dot(t_mat[i].astype(BF16), block_diag(power[i])) for i in ids]
    uw = [_dot(block_diag(t_mat[i]),
               jnp.concatenate([v_beta[i], k_beta[i] * eg_b[i]], axis=1).astype(BF16)) for i in ids]
    q_dec = [qn[i] * eg_b[i] for i in ids]
    state = [state_ref[s, hd] for s, hd in chains]
    kd_uw = []
    for i in ids:
        per_chunk = []
        for c in range(MIX_CHUNKS):
            uw_pad = jnp.concatenate(
                [uw[i][cc * CHUNK:(cc + 1) * CHUNK] if cc == c else jnp.zeros((CHUNK, 2 * DN_HEAD_DIM), F32)
                 for cc in range(MIX_CHUNKS)], axis=0).astype(BF16)
            per_chunk.append(_dot(kd_t[i], uw_pad))
        kd_uw.append(per_chunk)
    o_inter = [[] for _ in ids]
    v_new_all = [[] for _ in ids]
    for c in range(MIX_CHUNKS):
        r0 = c * CHUNK
        for i in ids:
            u_c = uw[i][r0:r0 + CHUNK, :DN_HEAD_DIM]
            w_c = uw[i][r0:r0 + CHUNK, DN_HEAD_DIM:]
            lhs = jnp.concatenate(
                [kd_uw[i][c][:, DN_HEAD_DIM:], w_c, q_dec[i][r0:r0 + CHUNK]], axis=0).astype(BF16)
            prod = _dot(lhs, state[i].astype(BF16))
            v_new_all[i].append(u_c - prod[DN_HEAD_DIM:DN_HEAD_DIM + CHUNK])
            o_inter[i].append(prod[DN_HEAD_DIM + CHUNK:])
            g_last = eg_b[i][r0 + CHUNK - 1:r0 + CHUNK, :]
            state[i] = state[i] * g_last - prod[:DN_HEAD_DIM] + kd_uw[i][c][:, :DN_HEAD_DIM]
    o_chain = []
    for i, (s, hd) in enumerate(chains):
        state_ref[s, hd] = state[i]
        v_new_full = jnp.concatenate(v_new_all[i], axis=0).astype(BF16)
        o_chain.append(jnp.concatenate(o_inter[i], axis=0) + _dot(qk[i], v_new_full))
    o_all = jnp.concatenate(
        [jnp.concatenate(o_chain[s * DN_HEADS:(s + 1) * DN_HEADS], axis=1) for s in seqs], axis=0)
    o_all = o_all * lax.rsqrt(_head_sums(o_all * o_all, ones2) * (1.0 / DN_HEAD_DIM) + RMS_EPS)
    o_all = o_all * dng_ref[...] * _silu(z)
    y_dn = _slab_dot(o_all.astype(BF16), dnproj_bf)

    merged = (jnp.concatenate(gate_sig[:2], axis=1) * y_dn
              + jnp.concatenate(gate_sig[2:], axis=1) * y_pool)
    m = _slab_dot(merged.astype(BF16), wout_bf)
    o_ref[...] = _layer_norm(ALPHA * h + m, lng_ref[...], lnb_ref[...]).reshape(nb, ts, D_MODEL)


def _mixer_ln(h3d, win_t, convw, prm, dng, dnproj, poolw, pscale, pproj, wout, lng, lnb):
    b, s, _ = h3d.shape
    assert win_t.shape == (W_GATE0 + 2 * D_MODEL, D_MODEL) and 2 * DN_HEADS == SUBLANES
    assert dnproj.shape == pproj.shape == (PREP_ROWS, D_MODEL) and wout.shape == (D_MODEL, D_MODEL)
    operands = (win_t, convw, prm, dng, dnproj, poolw, pscale, pproj, wout, lng, lnb, *_mixer_masks())
    in_hbm = (win_t, dnproj, pproj, wout)
    block = (MIX_BATCH, MIX_ROWS, D_MODEL)
    slabs = D_MODEL // SLAB_COLS
    return pl.pallas_call(
        _mixer_kernel,
        grid=(b // MIX_BATCH, s // MIX_ROWS),
        in_specs=[pl.BlockSpec(block, lambda i, j: (i, j, 0))]
        + [pl.BlockSpec(memory_space=pl.ANY) if any(op is w for w in in_hbm) else _resident(op.shape)
           for op in operands],
        out_specs=pl.BlockSpec(block, lambda i, j: (i, j, 0)),
        out_shape=jax.ShapeDtypeStruct(h3d.shape, F32),
        scratch_shapes=[
            pltpu.VMEM((MIX_BATCH, DN_HEADS, DN_HEAD_DIM, DN_HEAD_DIM), F32),
            pltpu.VMEM((MIX_BATCH, SUBLANES, QKV_COLS), F32),
            pltpu.VMEM((MIX_BATCH, MAX_WINDOW, POOL_WIDTH), F32),
            pltpu.VMEM((len(W_PIECES), D_MODEL, PREP_ROWS), BF16),
            pltpu.VMEM((SUBLANES, D_MODEL), F32),
            pltpu.VMEM((slabs, PREP_ROWS, SLAB_COLS), BF16),
            pltpu.VMEM((slabs, PREP_ROWS, SLAB_COLS), BF16),
            pltpu.VMEM((slabs, D_MODEL, SLAB_COLS), BF16),
            pltpu.VMEM((2, PREP_ROWS, D_MODEL), F32),
            pltpu.SemaphoreType.DMA((3,)),
        ],
        compiler_params=pltpu.CompilerParams(
            dimension_semantics=("arbitrary", "arbitrary"), vmem_limit_bytes=MIX_VMEM_LIMIT),
        name="mixer_ln",
    )(h3d, *operands)


def _row(v):
    return v.reshape(1, -1).astype(F32)


def kernel(x, ffn_pre_w_gate, ffn_pre_w_up, ffn_pre_w_down, norm_pre_g, norm_pre_b, mix_w_in, mix_conv_w, dn_a_log, dn_dt_bias, dn_norm_g, dn_w_proj, pool_w, pool_scale, pool_w_proj, mix_w_out, norm_mix_g, norm_mix_b, ffn_post_w_gate, ffn_post_w_up, ffn_post_w_down, norm_post_g, norm_post_b):
    bsz, seq, _ = x.shape
    h = x
    for l in range(DEPTH):
        h = _ffn_ln(h.reshape(bsz * seq, D_MODEL),
                    ffn_pre_w_gate[l], ffn_pre_w_up[l], ffn_pre_w_down[l], _row(norm_pre_g[l]), _row(norm_pre_b[l]),
                    "ffn_ln_pre")

        lanes = MIX_BATCH * MIX_ROWS
        prm = jnp.zeros((2, SUBLANES, lanes), F32)
        prm = prm.at[0, DN_HEADS:2 * DN_HEADS].set(jnp.broadcast_to(dn_a_log[l][:, None], (DN_HEADS, lanes)))
        prm = prm.at[1, DN_HEADS:2 * DN_HEADS].set(jnp.broadcast_to(dn_dt_bias[l][:, None], (DN_HEADS, lanes)))
        h = _mixer_ln(
            h.reshape(bsz, seq, D_MODEL),
            jnp.swapaxes(mix_w_in[l], 0, 1),
            mix_conv_w[l].astype(F32), prm, _row(jnp.tile(dn_norm_g[l], DN_HEADS)),
            dn_w_proj[l], pool_w[l], _row(pool_scale[l]), pool_w_proj[l], mix_w_out[l],
            _row(norm_mix_g[l]), _row(norm_mix_b[l]))

        h = _ffn_ln(h.reshape(bsz * seq, D_MODEL),
                    ffn_post_w_gate[l], ffn_post_w_up[l], ffn_post_w_down[l], _row(norm_post_g[l]), _row(norm_post_b[l]),
                    "ffn_ln_post")
    return h.reshape(bsz, seq, D_MODEL)
```

```python
import numpy as np

import jax
import jax.numpy as jnp
from jax import lax
from jax.experimental import pallas as pl
from jax.experimental.pallas import tpu as pltpu

F32 = jnp.float32
BF16 = jnp.bfloat16

D_MODEL = 1024
D_FF = 2816
DN_HEADS = 4
DN_HEAD_DIM = 128
DN_WIDTH = DN_HEADS * DN_HEAD_DIM
CONV_WIDTH = 4
CHUNK = 64
POOL_WINDOWS = (2, 4, 8, 16)
POOL_GROUPS = 4
POOL_GROUP_DIM = 128
POOL_WIDTH = POOL_GROUPS * POOL_GROUP_DIM
QKV_COLS = 3 * DN_WIDTH
W_Z0 = QKV_COLS
W_BA0 = W_Z0 + DN_WIDTH
W_P0 = W_BA0 + 2 * DN_HEADS
W_GATE0 = W_P0 + POOL_WIDTH
DEPTH = 1
ALPHA = (2.0 * DEPTH) ** 0.25
LN_EPS = 1e-5
RMS_EPS = 1e-6

SUBLANES = 8
LANES = 128
MIB = 1024 * 1024

FFN_ROWS = 512
FFN_EPILOGUE_ROWS = 64
FFN_COLS = 256
MIX_ROWS = 256
MIX_CHUNKS = MIX_ROWS // CHUNK
MIX_BATCH = 2
MAX_WINDOW = max(POOL_WINDOWS)
PREP_ROWS = 512
SLAB_COLS = 512
W_PIECES = (0, DN_WIDTH, 2 * DN_WIDTH, W_Z0, W_P0) + tuple(W_GATE0 + j * DN_WIDTH for j in range(4))
FFN_VMEM_LIMIT = 58 * MIB
MIX_VMEM_LIMIT = 56 * MIB


def _dot(a, b, **kw):
    return jnp.dot(a, b, preferred_element_type=F32, **kw)


def _dot_nt(a, b):
    return lax.dot_general(a, b, (((1,), (1,)), ((), ())), preferred_element_type=F32)


def _slab_dot(a, w_slabs):
    return jnp.concatenate([_dot(a, w_slabs[j]) for j in range(w_slabs.shape[0])], axis=1)


def _layer_norm(y, g, b):
    mu = jnp.mean(y, axis=-1, keepdims=True)
    yc = y - mu
    var = jnp.mean(yc * yc, axis=-1, keepdims=True)
    return yc * lax.rsqrt(var + LN_EPS) * g + b


def _sigmoid(x):
    return 0.5 * jnp.tanh(0.5 * x) + 0.5


def _silu(x):
    half = 0.5 * x
    return half + half * jnp.tanh(half)


def _head_sums(sq, ones2):
    sq = sq.astype(BF16)
    width = ones2.shape[0]
    return jnp.concatenate(
        [_dot(sq[:, i * width:(i + 1) * width], ones2) for i in range(sq.shape[1] // width)], axis=1)


def _issued_after(value, anchor):
    never = (anchor == anchor) & (anchor != anchor)
    return value + jnp.where(never, 1.0, 0.0)


def _ffn_ln_kernel(x_ref, x_prev_ref, wg_hbm, wu_hbm, wd_hbm, g_ref, b_ref, o_ref,
                   acc_keep, wg_ref, wu_ref, wd_ref, w_sem):
    i = pl.program_id(0)
    last = pl.num_programs(0) - 1
    pieces = FFN_ROWS // FFN_EPILOGUE_ROWS
    n_chunks = D_FF // FFN_COLS
    assert pieces + 2 <= n_chunks

    def weight_copies(c):
        span = pl.ds(c * FFN_COLS, FFN_COLS)
        return (pltpu.make_async_copy(wg_hbm.at[:, span], wg_ref.at[:, span], w_sem.at[0, c]),
                pltpu.make_async_copy(wu_hbm.at[:, span], wu_ref.at[:, span], w_sem.at[1, c]),
                pltpu.make_async_copy(wd_hbm.at[span, :], wd_ref.at[span, :], w_sem.at[2, c]))

    def epilogue_piece(k, anchor=None):
        rows = slice(k * FFN_EPILOGUE_ROWS, (k + 1) * FFN_EPILOGUE_ROWS)
        x_rows = x_prev_ref[rows, :]
        if anchor is not None:
            x_rows = _issued_after(x_rows, anchor)
        y = ALPHA * x_rows + 0.5 * acc_keep[rows, :]
        o_ref[rows, :] = _layer_norm(y, g_ref[...], b_ref[...])

    def tile_matmuls(first_step):
        x = x_ref[...]
        xb = x.astype(BF16)
        acc = jnp.zeros(x.shape, F32)
        for c in range(n_chunks):
            if first_step:
                for copy in weight_copies(c):
                    copy.wait()
            cols = slice(c * FFN_COLS, (c + 1) * FFN_COLS)
            gate = _dot(xb, wg_ref[:, cols].astype(BF16))
            up = _dot(xb, wu_ref[:, cols].astype(BF16))
            acc = acc + _dot((_silu(gate) * up).astype(BF16), wd_ref[cols, :].astype(BF16))
            if 2 <= c < pieces + 2:
                epilogue_piece(c - 2, acc[0:1, :])
        acc_keep[...] = acc

    @pl.when(i == 0)
    def _():
        acc_keep[...] = jnp.zeros(acc_keep.shape, F32)
        for c in range(n_chunks):
            for copy in weight_copies(c):
                copy.start()
        tile_matmuls(first_step=True)

    @pl.when((i > 0) & (i < last))
    def _():
        tile_matmuls(first_step=False)

    @pl.when(i == last)
    def _():
        for k in range(pieces):
            epilogue_piece(k)


def _resident(shape):
    return pl.BlockSpec(shape, lambda *_: (0,) * len(shape), pipeline_mode=pl.Buffered(1))


def _ffn_ln(x2d, wg, wu, wd, g, b, name):
    t = x2d.shape[0]
    tiles = t // FFN_ROWS
    return pl.pallas_call(
        _ffn_ln_kernel,
        grid=(tiles + 1,),
        in_specs=[
            pl.BlockSpec((FFN_ROWS, D_MODEL), lambda i: (jnp.minimum(i, tiles - 1), 0)),
            pl.BlockSpec((FFN_ROWS, D_MODEL), lambda i: (jnp.maximum(i - 1, 0), 0)),
            pl.BlockSpec(memory_space=pl.ANY),
            pl.BlockSpec(memory_space=pl.ANY),
            pl.BlockSpec(memory_space=pl.ANY),
            _resident((1, D_MODEL)),
            _resident((1, D_MODEL)),
        ],
        out_specs=pl.BlockSpec((FFN_ROWS, D_MODEL), lambda i: (jnp.maximum(i - 1, 0), 0)),
        out_shape=jax.ShapeDtypeStruct((t, D_MODEL), F32),
        scratch_shapes=[
            pltpu.VMEM((FFN_ROWS, D_MODEL), F32),
            pltpu.VMEM(wg.shape, F32),
            pltpu.VMEM(wu.shape, F32),
            pltpu.VMEM(wd.shape, F32),
            pltpu.SemaphoreType.DMA((3, D_FF // FFN_COLS)),
        ],
        compiler_params=pltpu.CompilerParams(
            dimension_semantics=("arbitrary",), vmem_limit_bytes=FFN_VMEM_LIMIT),
        name=name,
    )(x2d, x2d, wg, wu, wd, g, b)


def _mixer_masks():
    r = np.arange(MIX_ROWS)[:, None]
    c = np.arange(MIX_ROWS)[None, :]
    block_diag = (r // CHUNK == c // CHUNK)
    in_win = np.stack([(c <= r) & (c > r - w) for w in POOL_WINDOWS])
    rt = np.arange(MAX_WINDOW)[:, None]
    ct = np.arange(MAX_WINDOW)[None, :]
    in_tail = np.stack([(ct - MAX_WINDOW > rt - w) for w in POOL_WINDOWS])
    head_pair_ones = (r // DN_HEAD_DIM == c // DN_HEAD_DIM)
    as_bf16 = lambda m: jnp.asarray(m.astype(np.float32), dtype=BF16)
    return as_bf16(block_diag), as_bf16(in_win), as_bf16(in_tail), as_bf16(head_pair_ones)


def _mixer_kernel(h_ref, win_hbm, convw_ref, prm_ref, dng_ref,
                  dnproj_hbm, poolw_ref, pscale_ref, pproj_hbm, wout_hbm, lng_ref, lnb_ref,
                  bdmask_ref, inwin_ref, intail_ref, ones2_ref,
                  o_ref, state_ref, xtail_ref, ptail_ref,
                  win_bf, wba_ref, dnproj_bf, pproj_bf, wout_bf, stage, prep_sem):
    ts = MIX_ROWS
    nb = MIX_BATCH
    tile = pl.program_id(1)
    seqs = range(nb)
    rows_of = lambda s: slice(s * ts, (s + 1) * ts)

    @pl.when((pl.program_id(0) == 0) & (tile == 0))
    def _():
        def store_transposed(k):
            def store(x):
                win_bf[k] = x.T.astype(BF16)
            return store

        def store_slabs(dst, row_block):
            def store(x):
                rows = slice(row_block * PREP_ROWS, (row_block + 1) * PREP_ROWS)
                for j in range(D_MODEL // SLAB_COLS):
                    dst[j, rows, :] = x[:, j * SLAB_COLS:(j + 1) * SLAB_COLS].astype(BF16)
            return store

        jobs = [(win_hbm.at[pl.ds(first, PREP_ROWS), :], store_transposed(k))
                for k, first in enumerate(W_PIECES)]
        jobs.append((dnproj_hbm, store_slabs(dnproj_bf, 0)))

        def store_pool_projection(x):
            folded = []
            for g in range(POOL_GROUPS):
                rows = slice(g * POOL_GROUP_DIM, (g + 1) * POOL_GROUP_DIM)
                folded.append(_dot(poolw_ref[g] * pscale_ref[:, rows], x[rows, :],
                                   precision=lax.Precision.HIGHEST))
            store_slabs(pproj_bf, 0)(jnp.concatenate(folded, axis=0))

        jobs.append((pproj_hbm, store_pool_projection))
        jobs += [(wout_hbm.at[pl.ds(r * PREP_ROWS, PREP_ROWS), :], store_slabs(wout_bf, r))
                 for r in range(D_MODEL // PREP_ROWS)]
        ba_copy = pltpu.make_async_copy(win_hbm.at[pl.ds(W_BA0, SUBLANES), :], wba_ref, prep_sem.at[2])
        ba_copy.start()
        copies = [pltpu.make_async_copy(src, stage.at[n % 2], prep_sem.at[n % 2])
                  for n, (src, _) in enumerate(jobs)]
        copies[0].start()
        for n, (_, store) in enumerate(jobs):
            if n + 1 < len(jobs):
                copies[n + 1].start()
            copies[n].wait()
            store(stage[n % 2])
        ba_copy.wait()

    @pl.when(tile == 0)
    def _():
        state_ref[...] = jnp.zeros(state_ref.shape, F32)
        xtail_ref[...] = jnp.zeros(xtail_ref.shape, F32)
        ptail_ref[...] = jnp.zeros(ptail_ref.shape, F32)

    ones2 = ones2_ref[...]
    bdmask = bdmask_ref[...]
    prow = lax.broadcasted_iota(jnp.int32, (CHUNK, ts), 0)
    plane = lax.broadcasted_iota(jnp.int32, (CHUNK, ts), 1)
    pcol = plane % CHUNK
    in_block = [plane // CHUNK == c for c in range(MIX_CHUNKS)]
    causal_p = prow >= pcol
    strict_p = prow > pcol
    eye_p = (prow == pcol).astype(F32)
    low_lanes = lax.broadcasted_iota(jnp.int32, (CHUNK, LANES), 1) < CHUNK
    t_abs = tile * ts + lax.broadcasted_iota(jnp.int32, (ts, POOL_GROUP_DIM), 0)

    def column_broadcast(row):
        return jnp.broadcast_to(row, (DN_HEAD_DIM, ts)).T

    def pack(full):
        out = full[0:CHUNK]
        for c in range(1, MIX_CHUNKS):
            out = jnp.where(in_block[c], full[c * CHUNK:(c + 1) * CHUNK], out)
        return out

    def block_diag(packed):
        return jnp.concatenate([packed.astype(BF16)] * MIX_CHUNKS, axis=0) * bdmask

    h = h_ref[...].reshape(nb * ts, D_MODEL)
    hb = h.astype(BF16)

    def project(first_col, width):
        assert width == PREP_ROWS
        return _dot(hb, win_bf[W_PIECES.index(first_col)])

    def write_strength_and_decay():
        ba_t = _dot_nt(wba_ref[...].astype(BF16), hb)
        beta = _sigmoid(ba_t)
        a_shift = ba_t + prm_ref[1]
        softplus = jnp.maximum(a_shift, 0.0) + jnp.log1p(jnp.exp(-jnp.abs(a_shift)))
        g = -jnp.exp(prm_ref[0]) * softplus
        lane_in_chunk = lax.broadcasted_iota(jnp.int32, (SUBLANES, nb * ts), 1) % CHUNK
        shift = 1
        while shift < CHUNK:
            rolled = jnp.concatenate(
                [pltpu.roll(g[:, i * LANES:(i + 1) * LANES], shift, 1) for i in range(nb * ts // LANES)],
                axis=1)
            g = g + jnp.where(lane_in_chunk >= shift, rolled, 0.0)
            shift *= 2
        return beta, g

    def gate_group(j):
        return _sigmoid(project(W_GATE0 + j * DN_WIDTH, DN_WIDTH))

    qkv = []
    gate_sig = []
    for part in range(3):
        cols = slice(part * DN_WIDTH, (part + 1) * DN_WIDTH)
        raw = project(part * DN_WIDTH, DN_WIDTH)
        cw = convw_ref[:, cols]
        conv = []
        for s in seqs:
            raw_s = raw[rows_of(s)]
            ext = jnp.concatenate([xtail_ref[s, :, cols], raw_s], axis=0)
            xtail_ref[s, :, cols] = raw_s[ts - SUBLANES:, :]
            y = pltpu.roll(ext, CONV_WIDTH - 1, 0)[SUBLANES:] * cw[0:1]
            for tap in range(1, CONV_WIDTH - 1):
                y = y + pltpu.roll(ext, CONV_WIDTH - 1 - tap, 0)[SUBLANES:] * cw[tap:tap + 1]
            conv.append(y + raw_s * cw[CONV_WIDTH - 1:CONV_WIDTH])
        qkv.append(_silu(jnp.concatenate(conv, axis=0)))
        gate_sig.append(gate_group(part))
        if part == 1:
            beta_t, gr = write_strength_and_decay()
    q_all = qkv[0] * (lax.rsqrt(_head_sums(qkv[0] * qkv[0], ones2) + RMS_EPS) * (DN_HEAD_DIM ** -0.5))
    k_all = qkv[1] * lax.rsqrt(_head_sums(qkv[1] * qkv[1], ones2) + RMS_EPS)
    v_all = qkv[2]

    p = project(W_P0, POOL_WIDTH)
    pooled = []
    for s in seqs:
        p_s = p[rows_of(s)]
        tail = ptail_ref[s]
        ptail_ref[s] = p_s[ts - MAX_WINDOW:, :]
        pooled_s = []
        for gi, win in enumerate(POOL_WINDOWS):
            lo = gi * POOL_GROUP_DIM
            pg = p_s[:, lo:lo + POOL_GROUP_DIM]
            wsum = _dot(inwin_ref[gi], pg.astype(BF16))
            top = wsum[:MAX_WINDOW] + _dot(intail_ref[gi], tail[:, lo:lo + POOL_GROUP_DIM].astype(BF16))
            wsum = jnp.concatenate([top, wsum[MAX_WINDOW:]], axis=0)
            count = jnp.minimum(t_abs + 1, win).astype(F32)
            pooled_s.append(wsum / count - pg)
        pooled.append(jnp.concatenate(pooled_s, axis=1))
    pooled = jnp.concatenate(pooled, axis=0).astype(BF16)
    y_pool = _slab_dot(pooled, pproj_bf)
    z = project(W_Z0, DN_WIDTH)
    gate_sig.append(gate_group(3))

    chains = [(s, hd) for s in seqs for hd in range(DN_HEADS)]
    ids = range(len(chains))
    qn, kn_b, k_beta, v_beta, eg_b, kd_t, decay_p = [], [], [], [], [], [], []
    for s, hd in chains:
        lanes = slice(hd * DN_HEAD_DIM, (hd + 1) * DN_HEAD_DIM)
        qn.append(q_all[rows_of(s), lanes])
        kn = k_all[rows_of(s), lanes]
        v = v_all[rows_of(s), lanes]
        beta_b = column_broadcast(beta_t[hd:hd + 1, rows_of(s)])
        gr_row = gr[DN_HEADS + hd:DN_HEADS + hd + 1, rows_of(s)]
        gc_b = column_broadcast(gr_row)
        gc_p = jnp.concatenate(
            [jnp.where(low_lanes, gc_b[2 * i * CHUNK:(2 * i + 1) * CHUNK],
                       gc_b[(2 * i + 1) * CHUNK:(2 * i + 2) * CHUNK])
             for i in range(MIX_CHUNKS // 2)], axis=1)
        decay_p.append(jnp.exp(jnp.where(causal_p, gc_p - gr_row, -jnp.inf)))
        eg_b.append(jnp.exp(gc_b))
        gl_b = jnp.concatenate(
            [jnp.broadcast_to(gc_b[c * CHUNK + CHUNK - 1:(c + 1) * CHUNK, :], (CHUNK, DN_HEAD_DIM))
             for c in range(MIX_CHUNKS)], axis=0)
        k_beta.append(kn * beta_b)
        v_beta.append(v * beta_b)
        kn_b.append(kn.astype(BF16))
        kd_t.append((kn * jnp.exp(gl_b - gc_b)).T.astype(BF16))
    gram = [_dot_nt(jnp.concatenate([k_beta[i], qn[i]], axis=0).astype(BF16), kn_b[i]) for i in ids]
    lower = [pack(gram[i][:ts]) * jnp.where(strict_p, decay_p[i], 0.0) for i in ids]
    qk = [block_diag(pack(gram[i][ts:]) * decay_p[i]) for i in ids]
    t_mat = [eye_p - lower[i] for i in ids]
    power = [_dot(lower[i].astype(BF16), block_diag(lower[i])) for i in ids]
    for _ in range(4):
        both = [_dot(jnp.concatenate([power[i], t_mat[i]], axis=0).astype(BF16), block_diag(power[i]))
                for i in ids]
        power = [both[i][:CHUNK] for i in ids]
        t_mat = [t_mat[i] + both[i][CHUNK:] for i in ids]
    t_mat = [t_mat[i] + _dot(t_mat[i].astype(BF16), block_diag(power[i])) for i in ids]
    uw = [_dot(block_diag(t_mat[i]),
               jnp.concatenate([v_beta[i], k_beta[i] * eg_b[i]], axis=1).astype(BF16)) for i in ids]
    q_dec = [qn[i] * eg_b[i] for i in ids]
    state = [state_ref[s, hd] for s, hd in chains]
    kd_uw = []
    for i in ids:
        per_chunk = []
        for c in range(MIX_CHUNKS):
            uw_pad = jnp.concatenate(
                [uw[i][cc * CHUNK:(cc + 1) * CHUNK] if cc == c else jnp.zeros((CHUNK, 2 * DN_HEAD_DIM), F32)
                 for cc in range(MIX_CHUNKS)], axis=0).astype(BF16)
            per_chunk.append(_dot(kd_t[i], uw_pad))
        kd_uw.append(per_chunk)
    o_inter = [[] for _ in ids]
    v_new_all = [[] for _ in ids]
    for c in range(MIX_CHUNKS):
        r0 = c * CHUNK
        for i in ids:
            u_c = uw[i][r0:r0 + CHUNK, :DN_HEAD_DIM]
            w_c = uw[i][r0:r0 + CHUNK, DN_HEAD_DIM:]
            lhs = jnp.concatenate(
                [kd_uw[i][c][:, DN_HEAD_DIM:], w_c, q_dec[i][r0:r0 + CHUNK]], axis=0).astype(BF16)
            prod = _dot(lhs, state[i].astype(BF16))
            v_new_all[i].append(u_c - prod[DN_HEAD_DIM:DN_HEAD_DIM + CHUNK])
            o_inter[i].append(prod[DN_HEAD_DIM + CHUNK:])
            g_last = eg_b[i][r0 + CHUNK - 1:r0 + CHUNK, :]
            state[i] = state[i] * g_last - prod[:DN_HEAD_DIM] + kd_uw[i][c][:, :DN_HEAD_DIM]
    o_chain = []
    for i, (s, hd) in enumerate(chains):
        state_ref[s, hd] = state[i]
        v_new_full = jnp.concatenate(v_new_all[i], axis=0).astype(BF16)
        o_chain.append(jnp.concatenate(o_inter[i], axis=0) + _dot(qk[i], v_new_full))
    o_all = jnp.concatenate(
        [jnp.concatenate(o_chain[s * DN_HEADS:(s + 1) * DN_HEADS], axis=1) for s in seqs], axis=0)
    o_all = o_all * lax.rsqrt(_head_sums(o_all * o_all, ones2) * (1.0 / DN_HEAD_DIM) + RMS_EPS)
    o_all = o_all * dng_ref[...] * _silu(z)
    y_dn = _slab_dot(o_all.astype(BF16), dnproj_bf)

    merged = (jnp.concatenate(gate_sig[:2], axis=1) * y_dn
              + jnp.concatenate(gate_sig[2:], axis=1) * y_pool)
    m = _slab_dot(merged.astype(BF16), wout_bf)
    o_ref[...] = _layer_norm(ALPHA * h + m, lng_ref[...], lnb_ref[...]).reshape(nb, ts, D_MODEL)


def _mixer_ln(h3d, win_t, convw, prm, dng, dnproj, poolw, pscale, pproj, wout, lng, lnb):
    b, s, _ = h3d.shape
    assert win_t.shape == (W_GATE0 + 2 * D_MODEL, D_MODEL) and 2 * DN_HEADS == SUBLANES
    assert dnproj.shape == pproj.shape == (PREP_ROWS, D_MODEL) and wout.shape == (D_MODEL, D_MODEL)
    operands = (win_t, convw, prm, dng, dnproj, poolw, pscale, pproj, wout, lng, lnb, *_mixer_masks())
    in_hbm = (win_t, dnproj, pproj, wout)
    block = (MIX_BATCH, MIX_ROWS, D_MODEL)
    slabs = D_MODEL // SLAB_COLS
    return pl.pallas_call(
        _mixer_kernel,
        grid=(b // MIX_BATCH, s // MIX_ROWS),
        in_specs=[pl.BlockSpec(block, lambda i, j: (i, j, 0))]
        + [pl.BlockSpec(memory_space=pl.ANY) if any(op is w for w in in_hbm) else _resident(op.shape)
           for op in operands],
        out_specs=pl.BlockSpec(block, lambda i, j: (i, j, 0)),
        out_shape=jax.ShapeDtypeStruct(h3d.shape, F32),
        scratch_shapes=[
            pltpu.VMEM((MIX_BATCH, DN_HEADS, DN_HEAD_DIM, DN_HEAD_DIM), F32),
            pltpu.VMEM((MIX_BATCH, SUBLANES, QKV_COLS), F32),
            pltpu.VMEM((MIX_BATCH, MAX_WINDOW, POOL_WIDTH), F32),
            pltpu.VMEM((len(W_PIECES), D_MODEL, PREP_ROWS), BF16),
            pltpu.VMEM((SUBLANES, D_MODEL), F32),
            pltpu.VMEM((slabs, PREP_ROWS, SLAB_COLS), BF16),
            pltpu.VMEM((slabs, PREP_ROWS, SLAB_COLS), BF16),
            pltpu.VMEM((slabs, D_MODEL, SLAB_COLS), BF16),
            pltpu.VMEM((2, PREP_ROWS, D_MODEL), F32),
            pltpu.SemaphoreType.DMA((3,)),
        ],
        compiler_params=pltpu.CompilerParams(
            dimension_semantics=("arbitrary", "arbitrary"), vmem_limit_bytes=MIX_VMEM_LIMIT),
        name="mixer_ln",
    )(h3d, *operands)


def _row(v):
    return v.reshape(1, -1).astype(F32)


def kernel(x, ffn_pre_w_gate, ffn_pre_w_up, ffn_pre_w_down, norm_pre_g, norm_pre_b, mix_w_in, mix_conv_w, dn_a_log, dn_dt_bias, dn_norm_g, dn_w_proj, pool_w, pool_scale, pool_w_proj, mix_w_out, norm_mix_g, norm_mix_b, ffn_post_w_gate, ffn_post_w_up, ffn_post_w_down, norm_post_g, norm_post_b):
    bsz, seq, _ = x.shape
    h = x
    for l in range(DEPTH):
        h = _ffn_ln(h.reshape(bsz * seq, D_MODEL),
                    ffn_pre_w_gate[l], ffn_pre_w_up[l], ffn_pre_w_down[l], _row(norm_pre_g[l]), _row(norm_pre_b[l]),
                    "ffn_ln_pre")

        lanes = MIX_BATCH * MIX_ROWS
        prm = jnp.zeros((2, SUBLANES, lanes), F32)
        prm = prm.at[0, DN_HEADS:2 * DN_HEADS].set(jnp.broadcast_to(dn_a_log[l][:, None], (DN_HEADS, lanes)))
        prm = prm.at[1, DN_HEADS:2 * DN_HEADS].set(jnp.broadcast_to(dn_dt_bias[l][:, None], (DN_HEADS, lanes)))
        h = _mixer_ln(
            h.reshape(bsz, seq, D_MODEL),
            jnp.swapaxes(mix_w_in[l], 0, 1),
            mix_conv_w[l].astype(F32), prm, _row(jnp.tile(dn_norm_g[l], DN_HEADS)),
            dn_w_proj[l], pool_w[l], _row(pool_scale[l]), pool_w_proj[l], mix_w_out[l],
            _row(norm_mix_g[l]), _row(norm_mix_b[l]))

        h = _ffn_ln(h.reshape(bsz * seq, D_MODEL),
                    ffn_post_w_gate[l], ffn_post_w_up[l], ffn_post_w_down[l], _row(norm_post_g[l]), _row(norm_post_b[l]),
                    "ffn_ln_post")
    return h.reshape(bsz, seq, D_MODEL)
```

```python
import numpy as np

import jax
import jax.numpy as jnp
from jax import lax
from jax.experimental import pallas as pl
from jax.experimental.pallas import tpu as pltpu

F32 = jnp.float32
BF16 = jnp.bfloat16

D_MODEL = 1024
D_FF = 2816
DN_HEADS = 4
DN_HEAD_DIM = 128
DN_WIDTH = DN_HEADS * DN_HEAD_DIM
CONV_WIDTH = 4
CHUNK = 64
POOL_WINDOWS = (2, 4, 8, 16)
POOL_GROUPS = 4
POOL_GROUP_DIM = 128
POOL_WIDTH = POOL_GROUPS * POOL_GROUP_DIM
QKV_COLS = 3 * DN_WIDTH
W_Z0 = QKV_COLS
W_BA0 = W_Z0 + DN_WIDTH
W_P0 = W_BA0 + 2 * DN_HEADS
W_GATE0 = W_P0 + POOL_WIDTH
DEPTH = 1
ALPHA = (2.0 * DEPTH) ** 0.25
LN_EPS = 1e-5
RMS_EPS = 1e-6

SUBLANES = 8
LANES = 128
MIB = 1024 * 1024

FFN_ROWS = 512
FFN_EPILOGUE_ROWS = 64
FFN_COLS = 256
MIX_ROWS = 256
MIX_CHUNKS = MIX_ROWS // CHUNK
MIX_BATCH = 2
MAX_WINDOW = max(POOL_WINDOWS)
PREP_ROWS = 512
SLAB_COLS = 512
GATE_GROUPS = 2 * D_MODEL // DN_WIDTH
W_PIECES = ((0, DN_WIDTH, 2 * DN_WIDTH, W_Z0, W_P0)
            + tuple(W_GATE0 + j * DN_WIDTH for j in range(GATE_GROUPS)))
QKV_PARTS = QKV_COLS // DN_WIDTH
NEUMANN_LEVELS = CHUNK.bit_length() - 1
FFN_VMEM_LIMIT = 58 * MIB
MIX_VMEM_LIMIT = 56 * MIB


def _dot(a, b, **kw):
    return jnp.dot(a, b, preferred_element_type=F32, **kw)


def _dot_nt(a, b):
    return lax.dot_general(a, b, (((1,), (1,)), ((), ())), preferred_element_type=F32)


def _slab_dot(a, w_slabs):
    return jnp.concatenate([_dot(a, w_slabs[j]) for j in range(w_slabs.shape[0])], axis=1)


def _layer_norm(y, g, b):
    mu = jnp.mean(y, axis=-1, keepdims=True)
    yc = y - mu
    var = jnp.mean(yc * yc, axis=-1, keepdims=True)
    return yc * lax.rsqrt(var + LN_EPS) * g + b


def _sigmoid(x):
    return 0.5 * jnp.tanh(0.5 * x) + 0.5


def _silu(x):
    half = 0.5 * x
    return half + half * jnp.tanh(half)


def _head_sums(sq, ones2):
    sq = sq.astype(BF16)
    width = ones2.shape[0]
    return jnp.concatenate(
        [_dot(sq[:, i * width:(i + 1) * width], ones2) for i in range(sq.shape[1] // width)], axis=1)


def _issued_after(value, anchor):
    never = (anchor == anchor) & (anchor != anchor)
    return value + jnp.where(never, 1.0, 0.0)


def _ffn_ln_kernel(x_ref, x_prev_ref, wg_hbm, wu_hbm, wd_hbm, g_ref, b_ref, o_ref,
                   acc_keep, wg_bf, wu_bf, wd_bf, wg_stage, wu_stage, wd_stage, w_sem):
    i = pl.program_id(0)
    last = pl.num_programs(0) - 1
    pieces = FFN_ROWS // FFN_EPILOGUE_ROWS
    n_chunks = D_FF // FFN_COLS
    assert pieces + 2 <= n_chunks

    def weight_copies(c):
        span = pl.ds(c * FFN_COLS, FFN_COLS)
        slot = c % 2
        return (pltpu.make_async_copy(wg_hbm.at[:, span], wg_stage.at[slot], w_sem.at[0, slot]),
                pltpu.make_async_copy(wu_hbm.at[:, span], wu_stage.at[slot], w_sem.at[1, slot]),
                pltpu.make_async_copy(wd_hbm.at[span, :], wd_stage.at[slot], w_sem.at[2, slot]))

    def keep_chunk(c):
        cols = slice(c * FFN_COLS, (c + 1) * FFN_COLS)
        slot = c % 2
        wg_bf[:, cols] = wg_stage[slot].astype(BF16)
        wu_bf[:, cols] = wu_stage[slot].astype(BF16)
        down = wd_stage[slot]
        for j in range(D_MODEL // SLAB_COLS):
            wd_bf[j, cols, :] = down[:, j * SLAB_COLS:(j + 1) * SLAB_COLS].astype(BF16)

    def epilogue_piece(k, anchor=None):
        rows = slice(k * FFN_EPILOGUE_ROWS, (k + 1) * FFN_EPILOGUE_ROWS)
        x_rows = x_prev_ref[rows, :]
        if anchor is not None:
            x_rows = _issued_after(x_rows, anchor)
        y = ALPHA * x_rows + 0.5 * acc_keep[rows, :]
        o_ref[rows, :] = _layer_norm(y, g_ref[...], b_ref[...])

    def tile_matmuls(first_step):
        x = x_ref[...]
        xb = x.astype(BF16)
        acc = jnp.zeros(x.shape, F32)
        for c in range(n_chunks):
            if first_step:
                if c + 1 < n_chunks:
                    for copy in weight_copies(c + 1):
                        copy.start()
                for copy in weight_copies(c):
                    copy.wait()
                keep_chunk(c)
            cols = slice(c * FFN_COLS, (c + 1) * FFN_COLS)
            gate = _dot(xb, wg_bf[:, cols])
            up = _dot(xb, wu_bf[:, cols])
            act = (_silu(gate) * up).astype(BF16)
            acc = acc + jnp.concatenate(
                [_dot(act, wd_bf[j, cols, :]) for j in range(D_MODEL // SLAB_COLS)], axis=1)
            if 2 <= c < pieces + 2:
                epilogue_piece(c - 2, acc[0:1, :])
        acc_keep[...] = acc

    @pl.when(i == 0)
    def _():
        acc_keep[...] = jnp.zeros(acc_keep.shape, F32)
        for copy in weight_copies(0):
            copy.start()
        tile_matmuls(first_step=True)

    @pl.when((i > 0) & (i < last))
    def _():
        tile_matmuls(first_step=False)

    @pl.when(i == last)
    def _():
        for k in range(pieces):
            epilogue_piece(k)


def _resident(shape):
    return pl.BlockSpec(shape, lambda *_: (0,) * len(shape), pipeline_mode=pl.Buffered(1))


def _ffn_ln(x2d, wg, wu, wd, g, b, name):
    t = x2d.shape[0]
    tiles = t // FFN_ROWS
    return pl.pallas_call(
        _ffn_ln_kernel,
        grid=(tiles + 1,),
        in_specs=[
            pl.BlockSpec((FFN_ROWS, D_MODEL), lambda i: (jnp.minimum(i, tiles - 1), 0)),
            pl.BlockSpec((FFN_ROWS, D_MODEL), lambda i: (jnp.maximum(i - 1, 0), 0)),
            pl.BlockSpec(memory_space=pl.ANY),
            pl.BlockSpec(memory_space=pl.ANY),
            pl.BlockSpec(memory_space=pl.ANY),
            _resident((1, D_MODEL)),
            _resident((1, D_MODEL)),
        ],
        out_specs=pl.BlockSpec((FFN_ROWS, D_MODEL), lambda i: (jnp.maximum(i - 1, 0), 0)),
        out_shape=jax.ShapeDtypeStruct((t, D_MODEL), F32),
        scratch_shapes=[
            pltpu.VMEM((FFN_ROWS, D_MODEL), F32),
            pltpu.VMEM(wg.shape, BF16),
            pltpu.VMEM(wu.shape, BF16),
            pltpu.VMEM((D_MODEL // SLAB_COLS, D_FF, SLAB_COLS), BF16),
            pltpu.VMEM((2, D_MODEL, FFN_COLS), F32),
            pltpu.VMEM((2, D_MODEL, FFN_COLS), F32),
            pltpu.VMEM((2, FFN_COLS, D_MODEL), F32),
            pltpu.SemaphoreType.DMA((3, 2)),
        ],
        compiler_params=pltpu.CompilerParams(
            dimension_semantics=("arbitrary",), vmem_limit_bytes=FFN_VMEM_LIMIT),
        name=name,
    )(x2d, x2d, wg, wu, wd, g, b)


def _mixer_masks():
    r = np.arange(MIX_ROWS)[:, None]
    c = np.arange(MIX_ROWS)[None, :]
    block_diag = (r // CHUNK == c // CHUNK)
    in_win = np.stack([(c <= r) & (c > r - w) for w in POOL_WINDOWS])
    rt = np.arange(MAX_WINDOW)[:, None]
    ct = np.arange(MAX_WINDOW)[None, :]
    in_tail = np.stack([(ct - MAX_WINDOW > rt - w) for w in POOL_WINDOWS])
    head_pair_ones = (r // DN_HEAD_DIM == c // DN_HEAD_DIM)
    as_bf16 = lambda m: jnp.asarray(m.astype(np.float32), dtype=BF16)
    return as_bf16(block_diag), as_bf16(in_win), as_bf16(in_tail), as_bf16(head_pair_ones)


def _mixer_kernel(h_ref, win_hbm, convw_ref, prm_ref, dng_ref,
                  dnproj_hbm, poolw_ref, pscale_ref, pproj_hbm, wout_hbm, lng_ref, lnb_ref,
                  bdmask_ref, inwin_ref, intail_ref, ones2_ref,
                  o_ref, state_ref, xtail_ref, ptail_ref,
                  win_bf, wba_ref, dnproj_bf, pproj_bf, wout_bf, stage, prep_sem):
    ts = MIX_ROWS
    nb = MIX_BATCH
    tile = pl.program_id(1)
    seqs = range(nb)
    rows_of = lambda s: slice(s * ts, (s + 1) * ts)

    @pl.when((pl.program_id(0) == 0) & (tile == 0))
    def _():
        def store_transposed(k):
            def store(x):
                win_bf[k] = x.T.astype(BF16)
            return store

        def store_slabs(dst, row_block):
            def store(x):
                rows = slice(row_block * PREP_ROWS, (row_block + 1) * PREP_ROWS)
                for j in range(D_MODEL // SLAB_COLS):
                    dst[j, rows, :] = x[:, j * SLAB_COLS:(j + 1) * SLAB_COLS].astype(BF16)
            return store

        jobs = [(win_hbm.at[pl.ds(first, PREP_ROWS), :], store_transposed(k))
                for k, first in enumerate(W_PIECES)]
        jobs.append((dnproj_hbm, store_slabs(dnproj_bf, 0)))

        def store_pool_projection(x):
            folded = []
            for g in range(POOL_GROUPS):
                rows = slice(g * POOL_GROUP_DIM, (g + 1) * POOL_GROUP_DIM)
                folded.append(_dot(poolw_ref[g] * pscale_ref[:, rows], x[rows, :],
                                   precision=lax.Precision.HIGHEST))
            store_slabs(pproj_bf, 0)(jnp.concatenate(folded, axis=0))

        jobs.append((pproj_hbm, store_pool_projection))
        jobs += [(wout_hbm.at[pl.ds(r * PREP_ROWS, PREP_ROWS), :], store_slabs(wout_bf, r))
                 for r in range(D_MODEL // PREP_ROWS)]
        ba_copy = pltpu.make_async_copy(win_hbm.at[pl.ds(W_BA0, SUBLANES), :], wba_ref, prep_sem.at[2])
        ba_copy.start()
        copies = [pltpu.make_async_copy(src, stage.at[n % 2], prep_sem.at[n % 2])
                  for n, (src, _) in enumerate(jobs)]
        copies[0].start()
        for n, (_, store) in enumerate(jobs):
            if n + 1 < len(jobs):
                copies[n + 1].start()
            copies[n].wait()
            store(stage[n % 2])
        ba_copy.wait()

    @pl.when(tile == 0)
    def _():
        state_ref[...] = jnp.zeros(state_ref.shape, F32)
        xtail_ref[...] = jnp.zeros(xtail_ref.shape, F32)
        ptail_ref[...] = jnp.zeros(ptail_ref.shape, F32)

    ones2 = ones2_ref[...]
    bdmask = bdmask_ref[...]
    prow = lax.broadcasted_iota(jnp.int32, (CHUNK, ts), 0)
    plane = lax.broadcasted_iota(jnp.int32, (CHUNK, ts), 1)
    pcol = plane % CHUNK
    in_block = [plane // CHUNK == c for c in range(MIX_CHUNKS)]
    causal_p = prow >= pcol
    strict_p = prow > pcol
    eye_p = (prow == pcol).astype(F32)
    low_lanes = lax.broadcasted_iota(jnp.int32, (CHUNK, LANES), 1) < CHUNK
    t_abs = tile * ts + lax.broadcasted_iota(jnp.int32, (ts, POOL_GROUP_DIM), 0)

    def column_broadcast(row):
        return jnp.broadcast_to(row, (DN_HEAD_DIM, ts)).T

    def pack(full):
        out = full[0:CHUNK]
        for c in range(1, MIX_CHUNKS):
            out = jnp.where(in_block[c], full[c * CHUNK:(c + 1) * CHUNK], out)
        return out

    def block_diag(packed):
        return jnp.concatenate([packed.astype(BF16)] * MIX_CHUNKS, axis=0) * bdmask

    h = h_ref[...].reshape(nb * ts, D_MODEL)
    hb = h.astype(BF16)

    def project(first_col, width):
        assert width == PREP_ROWS
        return _dot(hb, win_bf[W_PIECES.index(first_col)])

    def write_strength_and_decay():
        ba_t = _dot_nt(wba_ref[...].astype(BF16), hb)
        beta = _sigmoid(ba_t)
        a_shift = ba_t + prm_ref[1]
        softplus = jnp.maximum(a_shift, 0.0) + jnp.log1p(jnp.exp(-jnp.abs(a_shift)))
        g = -jnp.exp(prm_ref[0]) * softplus
        lane_in_chunk = lax.broadcasted_iota(jnp.int32, (SUBLANES, nb * ts), 1) % CHUNK
        shift = 1
        while shift < CHUNK:
            rolled = jnp.concatenate(
                [pltpu.roll(g[:, i * LANES:(i + 1) * LANES], shift, 1) for i in range(nb * ts // LANES)],
                axis=1)
            g = g + jnp.where(lane_in_chunk >= shift, rolled, 0.0)
            shift *= 2
        return beta, g

    def gate_group(j):
        return _sigmoid(project(W_GATE0 + j * DN_WIDTH, DN_WIDTH))

    qkv = []
    gate_sig = []
    for part in range(QKV_PARTS):
        cols = slice(part * DN_WIDTH, (part + 1) * DN_WIDTH)
        raw = project(part * DN_WIDTH, DN_WIDTH)
        cw = convw_ref[:, cols]
        conv = []
        for s in seqs:
            raw_s = raw[rows_of(s)]
            ext = jnp.concatenate([xtail_ref[s, :, cols], raw_s], axis=0)
            xtail_ref[s, :, cols] = raw_s[ts - SUBLANES:, :]
            y = pltpu.roll(ext, CONV_WIDTH - 1, 0)[SUBLANES:] * cw[0:1]
            for tap in range(1, CONV_WIDTH - 1):
                y = y + pltpu.roll(ext, CONV_WIDTH - 1 - tap, 0)[SUBLANES:] * cw[tap:tap + 1]
            conv.append(y + raw_s * cw[CONV_WIDTH - 1:CONV_WIDTH])
        qkv.append(_silu(jnp.concatenate(conv, axis=0)))
        gate_sig.append(gate_group(part))
        if part == 1:
            beta_t, gr = write_strength_and_decay()
    q_all = qkv[0] * (lax.rsqrt(_head_sums(qkv[0] * qkv[0], ones2) + RMS_EPS) * (DN_HEAD_DIM ** -0.5))
    k_all = qkv[1] * lax.rsqrt(_head_sums(qkv[1] * qkv[1], ones2) + RMS_EPS)
    v_all = qkv[2]

    p = project(W_P0, POOL_WIDTH)
    pooled = []
    for s in seqs:
        p_s = p[rows_of(s)]
        tail = ptail_ref[s]
        ptail_ref[s] = p_s[ts - MAX_WINDOW:, :]
        pooled_s = []
        for gi, win in enumerate(POOL_WINDOWS):
            lo = gi * POOL_GROUP_DIM
            pg = p_s[:, lo:lo + POOL_GROUP_DIM]
            wsum = _dot(inwin_ref[gi], pg.astype(BF16))
            top = wsum[:MAX_WINDOW] + _dot(intail_ref[gi], tail[:, lo:lo + POOL_GROUP_DIM].astype(BF16))
            wsum = jnp.concatenate([top, wsum[MAX_WINDOW:]], axis=0)
            count = jnp.minimum(t_abs + 1, win).astype(F32)
            pooled_s.append(wsum / count - pg)
        pooled.append(jnp.concatenate(pooled_s, axis=1))
    pooled = jnp.concatenate(pooled, axis=0).astype(BF16)
    y_pool = _slab_dot(pooled, pproj_bf)
    z = project(W_Z0, DN_WIDTH)
    gate_sig += [gate_group(j) for j in range(QKV_PARTS, GATE_GROUPS)]

    chains = [(s, hd) for s in seqs for hd in range(DN_HEADS)]
    ids = range(len(chains))
    qn, kn_b, k_beta, v_beta, eg_b, kd_t, decay_p = [], [], [], [], [], [], []
    for s, hd in chains:
        lanes = slice(hd * DN_HEAD_DIM, (hd + 1) * DN_HEAD_DIM)
        qn.append(q_all[rows_of(s), lanes])
        kn = k_all[rows_of(s), lanes]
        v = v_all[rows_of(s), lanes]
        beta_b = column_broadcast(beta_t[hd:hd + 1, rows_of(s)])
        gr_row = gr[DN_HEADS + hd:DN_HEADS + hd + 1, rows_of(s)]
        gc_b = column_broadcast(gr_row)
        gc_p = jnp.concatenate(
            [jnp.where(low_lanes, gc_b[2 * i * CHUNK:(2 * i + 1) * CHUNK],
                       gc_b[(2 * i + 1) * CHUNK:(2 * i + 2) * CHUNK])
             for i in range(MIX_CHUNKS // 2)], axis=1)
        decay_p.append(jnp.exp(jnp.where(causal_p, gc_p - gr_row, -jnp.inf)))
        eg_b.append(jnp.exp(gc_b))
        gl_b = jnp.concatenate(
            [jnp.broadcast_to(gc_b[c * CHUNK + CHUNK - 1:(c + 1) * CHUNK, :], (CHUNK, DN_HEAD_DIM))
             for c in range(MIX_CHUNKS)], axis=0)
        k_beta.append(kn * beta_b)
        v_beta.append(v * beta_b)
        kn_b.append(kn.astype(BF16))
        kd_t.append((kn * jnp.exp(gl_b - gc_b)).T.astype(BF16))
    gram = [_dot_nt(jnp.concatenate([k_beta[i], qn[i]], axis=0).astype(BF16), kn_b[i]) for i in ids]
    lower = [pack(gram[i][:ts]) * jnp.where(strict_p, decay_p[i], 0.0) for i in ids]
    qk = [block_diag(pack(gram[i][ts:]) * decay_p[i]) for i in ids]
    t_mat = [eye_p - lower[i] for i in ids]
    power = [_dot(lower[i].astype(BF16), block_diag(lower[i])) for i in ids]
    for _ in range(NEUMANN_LEVELS - 2):
        both = [_dot(jnp.concatenate([power[i], t_mat[i]], axis=0).astype(BF16), block_diag(power[i]))
                for i in ids]
        power = [both[i][:CHUNK] for i in ids]
        t_mat = [t_mat[i] + both[i][CHUNK:] for i in ids]
    t_mat = [t_mat[i] + _dot(t_mat[i].astype(BF16), block_diag(power[i])) for i in ids]
    uw = [_dot(block_diag(t_mat[i]),
               jnp.concatenate([v_beta[i], k_beta[i] * eg_b[i]], axis=1).astype(BF16)) for i in ids]
    q_dec = [qn[i] * eg_b[i] for i in ids]
    state = [state_ref[s, hd] for s, hd in chains]
    kd_uw = []
    for i in ids:
        per_chunk = []
        for c in range(MIX_CHUNKS):
            uw_pad = jnp.concatenate(
                [uw[i][cc * CHUNK:(cc + 1) * CHUNK] if cc == c else jnp.zeros((CHUNK, 2 * DN_HEAD_DIM), F32)
                 for cc in range(MIX_CHUNKS)], axis=0).astype(BF16)
            per_chunk.append(_dot(kd_t[i], uw_pad))
        kd_uw.append(per_chunk)
    o_inter = [[] for _ in ids]
    v_new_all = [[] for _ in ids]
    for c in range(MIX_CHUNKS):
        r0 = c * CHUNK
        for i in ids:
            u_c = uw[i][r0:r0 + CHUNK, :DN_HEAD_DIM]
            w_c = uw[i][r0:r0 + CHUNK, DN_HEAD_DIM:]
            lhs = jnp.concatenate(
                [kd_uw[i][c][:, DN_HEAD_DIM:], w_c, q_dec[i][r0:r0 + CHUNK]], axis=0).astype(BF16)
            prod = _dot(lhs, state[i].astype(BF16))
            v_new_all[i].append(u_c - prod[DN_HEAD_DIM:DN_HEAD_DIM + CHUNK])
            o_inter[i].append(prod[DN_HEAD_DIM + CHUNK:])
            g_last = eg_b[i][r0 + CHUNK - 1:r0 + CHUNK, :]
            state[i] = state[i] * g_last - prod[:DN_HEAD_DIM] + kd_uw[i][c][:, :DN_HEAD_DIM]
    o_chain = []
    for i, (s, hd) in enumerate(chains):
        state_ref[s, hd] = state[i]
        v_new_full = jnp.concatenate(v_new_all[i], axis=0).astype(BF16)
        o_chain.append(jnp.concatenate(o_inter[i], axis=0) + _dot(qk[i], v_new_full))
    o_all = jnp.concatenate(
        [jnp.concatenate(o_chain[s * DN_HEADS:(s + 1) * DN_HEADS], axis=1) for s in seqs], axis=0)
    o_all = o_all * lax.rsqrt(_head_sums(o_all * o_all, ones2) * (1.0 / DN_HEAD_DIM) + RMS_EPS)
    o_all = o_all * dng_ref[...] * _silu(z)
    y_dn = _slab_dot(o_all.astype(BF16), dnproj_bf)

    half = GATE_GROUPS // 2
    merged = (jnp.concatenate(gate_sig[:half], axis=1) * y_dn
              + jnp.concatenate(gate_sig[half:], axis=1) * y_pool)
    m = _slab_dot(merged.astype(BF16), wout_bf)
    o_ref[...] = _layer_norm(ALPHA * h + m, lng_ref[...], lnb_ref[...]).reshape(nb, ts, D_MODEL)


def _mixer_ln(h3d, win_t, convw, prm, dng, dnproj, poolw, pscale, pproj, wout, lng, lnb):
    b, s, _ = h3d.shape
    assert win_t.shape == (W_GATE0 + 2 * D_MODEL, D_MODEL) and 2 * DN_HEADS == SUBLANES
    assert dnproj.shape == pproj.shape == (PREP_ROWS, D_MODEL) and wout.shape == (D_MODEL, D_MODEL)
    operands = (win_t, convw, prm, dng, dnproj, poolw, pscale, pproj, wout, lng, lnb, *_mixer_masks())
    in_hbm = (win_t, dnproj, pproj, wout)
    block = (MIX_BATCH, MIX_ROWS, D_MODEL)
    slabs = D_MODEL // SLAB_COLS
    return pl.pallas_call(
        _mixer_kernel,
        grid=(b // MIX_BATCH, s // MIX_ROWS),
        in_specs=[pl.BlockSpec(block, lambda i, j: (i, j, 0))]
        + [pl.BlockSpec(memory_space=pl.ANY) if any(op is w for w in in_hbm) else _resident(op.shape)
           for op in operands],
        out_specs=pl.BlockSpec(block, lambda i, j: (i, j, 0)),
        out_shape=jax.ShapeDtypeStruct(h3d.shape, F32),
        scratch_shapes=[
            pltpu.VMEM((MIX_BATCH, DN_HEADS, DN_HEAD_DIM, DN_HEAD_DIM), F32),
            pltpu.VMEM((MIX_BATCH, SUBLANES, QKV_COLS), F32),
            pltpu.VMEM((MIX_BATCH, MAX_WINDOW, POOL_WIDTH), F32),
            pltpu.VMEM((len(W_PIECES), D_MODEL, PREP_ROWS), BF16),
            pltpu.VMEM((SUBLANES, D_MODEL), F32),
            pltpu.VMEM((slabs, PREP_ROWS, SLAB_COLS), BF16),
            pltpu.VMEM((slabs, PREP_ROWS, SLAB_COLS), BF16),
            pltpu.VMEM((slabs, D_MODEL, SLAB_COLS), BF16),
            pltpu.VMEM((2, PREP_ROWS, D_MODEL), F32),
            pltpu.SemaphoreType.DMA((3,)),
        ],
        compiler_params=pltpu.CompilerParams(
            dimension_semantics=("arbitrary", "arbitrary"), vmem_limit_bytes=MIX_VMEM_LIMIT),
        name="mixer_ln",
    )(h3d, *operands)


def _row(v):
    return v.reshape(1, -1).astype(F32)


def kernel(x, ffn_pre_w_gate, ffn_pre_w_up, ffn_pre_w_down, norm_pre_g, norm_pre_b, mix_w_in, mix_conv_w, dn_a_log, dn_dt_bias, dn_norm_g, dn_w_proj, pool_w, pool_scale, pool_w_proj, mix_w_out, norm_mix_g, norm_mix_b, ffn_post_w_gate, ffn_post_w_up, ffn_post_w_down, norm_post_g, norm_post_b):
    bsz, seq, _ = x.shape
    h = x
    for l in range(DEPTH):
        h = _ffn_ln(h.reshape(bsz * seq, D_MODEL),
                    ffn_pre_w_gate[l], ffn_pre_w_up[l], ffn_pre_w_down[l], _row(norm_pre_g[l]), _row(norm_pre_b[l]),
                    "ffn_ln_pre")

        lanes = MIX_BATCH * MIX_ROWS
        prm = jnp.zeros((2, SUBLANES, lanes), F32)
        prm = prm.at[0, DN_HEADS:2 * DN_HEADS].set(jnp.broadcast_to(dn_a_log[l][:, None], (DN_HEADS, lanes)))
        prm = prm.at[1, DN_HEADS:2 * DN_HEADS].set(jnp.broadcast_to(dn_dt_bias[l][:, None], (DN_HEADS, lanes)))
        h = _mixer_ln(
            h.reshape(bsz, seq, D_MODEL),
            jnp.swapaxes(mix_w_in[l], 0, 1),
            mix_conv_w[l].astype(F32), prm, _row(jnp.tile(dn_norm_g[l], DN_HEADS)),
            dn_w_proj[l], pool_w[l], _row(pool_scale[l]), pool_w_proj[l], mix_w_out[l],
            _row(norm_mix_g[l]), _row(norm_mix_b[l]))

        h = _ffn_ln(h.reshape(bsz * seq, D_MODEL),
                    ffn_post_w_gate[l], ffn_post_w_up[l], ffn_post_w_down[l], _row(norm_post_g[l]), _row(norm_post_b[l]),
                    "ffn_ln_post")
    return h.reshape(bsz, seq, D_MODEL)
```

```python
import numpy as np

import jax
import jax.numpy as jnp
from jax import lax
from jax.experimental import pallas as pl
from jax.experimental.pallas import tpu as pltpu

F32 = jnp.float32
BF16 = jnp.bfloat16

D_MODEL = 1024
D_FF = 2816
DN_HEADS = 4
DN_HEAD_DIM = 128
DN_WIDTH = DN_HEADS * DN_HEAD_DIM
CONV_WIDTH = 4
CHUNK = 64
POOL_WINDOWS = (2, 4, 8, 16)
POOL_GROUPS = 4
POOL_GROUP_DIM = 128
POOL_WIDTH = POOL_GROUPS * POOL_GROUP_DIM
QKV_COLS = 3 * DN_WIDTH
W_Z0 = QKV_COLS
W_BA0 = W_Z0 + DN_WIDTH
W_P0 = W_BA0 + 2 * DN_HEADS
W_GATE0 = W_P0 + POOL_WIDTH
DEPTH = 1
ALPHA = (2.0 * DEPTH) ** 0.25
LN_EPS = 1e-5
RMS_EPS = 1e-6

SUBLANES = 8
LANES = 128
MIB = 1024 * 1024

FFN_ROWS = 512
FFN_EPILOGUE_ROWS = 64
FFN_COLS = 256
MIX_ROWS = 256
MIX_CHUNKS = MIX_ROWS // CHUNK
MIX_BATCH = 4
MAX_WINDOW = max(POOL_WINDOWS)
PREP_ROWS = 512
SLAB_COLS = 512
GATE_GROUPS = 2 * D_MODEL // DN_WIDTH
W_PIECES = ((0, DN_WIDTH, 2 * DN_WIDTH, W_Z0, W_P0)
            + tuple(W_GATE0 + j * DN_WIDTH for j in range(GATE_GROUPS)))
QKV_PARTS = QKV_COLS // DN_WIDTH
NEUMANN_LEVELS = CHUNK.bit_length() - 1
FFN_VMEM_LIMIT = 58 * MIB
MIX_VMEM_LIMIT = 62 * MIB


def _dot(a, b, **kw):
    return jnp.dot(a, b, preferred_element_type=F32, **kw)


def _dot_nt(a, b):
    return lax.dot_general(a, b, (((1,), (1,)), ((), ())), preferred_element_type=F32)


def _slab_dot(a, w_slabs):
    return jnp.concatenate([_dot(a, w_slabs[j]) for j in range(w_slabs.shape[0])], axis=1)


def _layer_norm(y, g, b):
    mu = jnp.mean(y, axis=-1, keepdims=True)
    yc = y - mu
    var = jnp.mean(yc * yc, axis=-1, keepdims=True)
    return yc * lax.rsqrt(var + LN_EPS) * g + b


def _sigmoid(x):
    return 0.5 * jnp.tanh(0.5 * x) + 0.5


def _silu(x):
    half = 0.5 * x
    return half + half * jnp.tanh(half)


def _head_sums(sq, ones2):
    sq = sq.astype(BF16)
    width = ones2.shape[0]
    return jnp.concatenate(
        [_dot(sq[:, i * width:(i + 1) * width], ones2) for i in range(sq.shape[1] // width)], axis=1)


def _issued_after(value, anchor):
    never = (anchor == anchor) & (anchor != anchor)
    return value + jnp.where(never, 1.0, 0.0)


def _ffn_ln_kernel(x_ref, x_prev_ref, wg_hbm, wu_hbm, wd_hbm, g_ref, b_ref, o_ref,
                   acc_keep, wg_bf, wu_bf, wd_bf, wg_stage, wu_stage, wd_stage, w_sem):
    i = pl.program_id(0)
    last = pl.num_programs(0) - 1
    pieces = FFN_ROWS // FFN_EPILOGUE_ROWS
    n_chunks = D_FF // FFN_COLS
    assert pieces + 2 <= n_chunks

    def weight_copies(c):
        span = pl.ds(c * FFN_COLS, FFN_COLS)
        slot = c % 2
        return (pltpu.make_async_copy(wg_hbm.at[:, span], wg_stage.at[slot], w_sem.at[0, slot]),
                pltpu.make_async_copy(wu_hbm.at[:, span], wu_stage.at[slot], w_sem.at[1, slot]),
                pltpu.make_async_copy(wd_hbm.at[span, :], wd_stage.at[slot], w_sem.at[2, slot]))

    def keep_chunk(c):
        cols = slice(c * FFN_COLS, (c + 1) * FFN_COLS)
        slot = c % 2
        wg_bf[:, cols] = wg_stage[slot].astype(BF16)
        wu_bf[:, cols] = wu_stage[slot].astype(BF16)
        down = wd_stage[slot]
        for j in range(D_MODEL // SLAB_COLS):
            wd_bf[j, cols, :] = down[:, j * SLAB_COLS:(j + 1) * SLAB_COLS].astype(BF16)

    def epilogue_piece(k, anchor=None):
        rows = slice(k * FFN_EPILOGUE_ROWS, (k + 1) * FFN_EPILOGUE_ROWS)
        x_rows = x_prev_ref[rows, :]
        if anchor is not None:
            x_rows = _issued_after(x_rows, anchor)
        y = ALPHA * x_rows + 0.5 * acc_keep[rows, :]
        o_ref[rows, :] = _layer_norm(y, g_ref[...], b_ref[...])

    def tile_matmuls(first_step):
        x = x_ref[...]
        xb = x.astype(BF16)
        acc = jnp.zeros(x.shape, F32)
        for c in range(n_chunks):
            if first_step:
                if c + 1 < n_chunks:
                    for copy in weight_copies(c + 1):
                        copy.start()
                for copy in weight_copies(c):
                    copy.wait()
                keep_chunk(c)
            cols = slice(c * FFN_COLS, (c + 1) * FFN_COLS)
            gate = _dot(xb, wg_bf[:, cols])
            up = _dot(xb, wu_bf[:, cols])
            act = (_silu(gate) * up).astype(BF16)
            acc = acc + jnp.concatenate(
                [_dot(act, wd_bf[j, cols, :]) for j in range(D_MODEL // SLAB_COLS)], axis=1)
            if 2 <= c < pieces + 2:
                epilogue_piece(c - 2, acc[0:1, :])
        acc_keep[...] = acc

    @pl.when(i == 0)
    def _():
        acc_keep[...] = jnp.zeros(acc_keep.shape, F32)
        for copy in weight_copies(0):
            copy.start()
        tile_matmuls(first_step=True)

    @pl.when((i > 0) & (i < last))
    def _():
        tile_matmuls(first_step=False)

    @pl.when(i == last)
    def _():
        for k in range(pieces):
            epilogue_piece(k)


def _resident(shape):
    return pl.BlockSpec(shape, lambda *_: (0,) * len(shape), pipeline_mode=pl.Buffered(1))


def _ffn_ln(x2d, wg, wu, wd, g, b, name):
    t = x2d.shape[0]
    tiles = t // FFN_ROWS
    return pl.pallas_call(
        _ffn_ln_kernel,
        grid=(tiles + 1,),
        in_specs=[
            pl.BlockSpec((FFN_ROWS, D_MODEL), lambda i: (jnp.minimum(i, tiles - 1), 0)),
            pl.BlockSpec((FFN_ROWS, D_MODEL), lambda i: (jnp.maximum(i - 1, 0), 0)),
            pl.BlockSpec(memory_space=pl.ANY),
            pl.BlockSpec(memory_space=pl.ANY),
            pl.BlockSpec(memory_space=pl.ANY),
            _resident((1, D_MODEL)),
            _resident((1, D_MODEL)),
        ],
        out_specs=pl.BlockSpec((FFN_ROWS, D_MODEL), lambda i: (jnp.maximum(i - 1, 0), 0)),
        out_shape=jax.ShapeDtypeStruct((t, D_MODEL), F32),
        scratch_shapes=[
            pltpu.VMEM((FFN_ROWS, D_MODEL), F32),
            pltpu.VMEM(wg.shape, BF16),
            pltpu.VMEM(wu.shape, BF16),
            pltpu.VMEM((D_MODEL // SLAB_COLS, D_FF, SLAB_COLS), BF16),
            pltpu.VMEM((2, D_MODEL, FFN_COLS), F32),
            pltpu.VMEM((2, D_MODEL, FFN_COLS), F32),
            pltpu.VMEM((2, FFN_COLS, D_MODEL), F32),
            pltpu.SemaphoreType.DMA((3, 2)),
        ],
        compiler_params=pltpu.CompilerParams(
            dimension_semantics=("arbitrary",), vmem_limit_bytes=FFN_VMEM_LIMIT),
        name=name,
    )(x2d, x2d, wg, wu, wd, g, b)


def _mixer_masks():
    r = np.arange(MIX_ROWS)[:, None]
    c = np.arange(MIX_ROWS)[None, :]
    block_diag = (r // CHUNK == c // CHUNK)
    in_win = np.stack([(c <= r) & (c > r - w) for w in POOL_WINDOWS])
    rt = np.arange(MAX_WINDOW)[:, None]
    ct = np.arange(MAX_WINDOW)[None, :]
    in_tail = np.stack([(ct - MAX_WINDOW > rt - w) for w in POOL_WINDOWS])
    head_pair_ones = (r // DN_HEAD_DIM == c // DN_HEAD_DIM)
    as_bf16 = lambda m: jnp.asarray(m.astype(np.float32), dtype=BF16)
    return as_bf16(block_diag), as_bf16(in_win), as_bf16(in_tail), as_bf16(head_pair_ones)


def _mixer_kernel(h_ref, win_hbm, convw_ref, prm_ref, dng_ref,
                  dnproj_hbm, poolw_ref, pscale_ref, pproj_hbm, wout_hbm, lng_ref, lnb_ref,
                  bdmask_ref, inwin_ref, intail_ref, ones2_ref,
                  o_ref, state_ref, xtail_ref, ptail_ref,
                  win_bf, wba_ref, dnproj_bf, pproj_bf, wout_bf, stage, prep_sem):
    ts = MIX_ROWS
    nb = MIX_BATCH
    tile = pl.program_id(1)
    seqs = range(nb)
    rows_of = lambda s: slice(s * ts, (s + 1) * ts)

    @pl.when((pl.program_id(0) == 0) & (tile == 0))
    def _():
        def store_transposed(k):
            def store(x):
                win_bf[k] = x.T.astype(BF16)
            return store

        def store_slabs(dst, row_block):
            def store(x):
                rows = slice(row_block * PREP_ROWS, (row_block + 1) * PREP_ROWS)
                for j in range(D_MODEL // SLAB_COLS):
                    dst[j, rows, :] = x[:, j * SLAB_COLS:(j + 1) * SLAB_COLS].astype(BF16)
            return store

        jobs = [(win_hbm.at[pl.ds(first, PREP_ROWS), :], store_transposed(k))
                for k, first in enumerate(W_PIECES)]
        jobs.append((dnproj_hbm, store_slabs(dnproj_bf, 0)))

        def store_pool_projection(x):
            folded = []
            for g in range(POOL_GROUPS):
                rows = slice(g * POOL_GROUP_DIM, (g + 1) * POOL_GROUP_DIM)
                folded.append(_dot(poolw_ref[g] * pscale_ref[:, rows], x[rows, :],
                                   precision=lax.Precision.HIGHEST))
            store_slabs(pproj_bf, 0)(jnp.concatenate(folded, axis=0))

        jobs.append((pproj_hbm, store_pool_projection))
        jobs += [(wout_hbm.at[pl.ds(r * PREP_ROWS, PREP_ROWS), :], store_slabs(wout_bf, r))
                 for r in range(D_MODEL // PREP_ROWS)]
        ba_copy = pltpu.make_async_copy(win_hbm.at[pl.ds(W_BA0, SUBLANES), :], wba_ref, prep_sem.at[2])
        ba_copy.start()
        copies = [pltpu.make_async_copy(src, stage.at[n % 2], prep_sem.at[n % 2])
                  for n, (src, _) in enumerate(jobs)]
        copies[0].start()
        for n, (_, store) in enumerate(jobs):
            if n + 1 < len(jobs):
                copies[n + 1].start()
            copies[n].wait()
            store(stage[n % 2])
        ba_copy.wait()

    @pl.when(tile == 0)
    def _():
        state_ref[...] = jnp.zeros(state_ref.shape, F32)
        xtail_ref[...] = jnp.zeros(xtail_ref.shape, F32)
        ptail_ref[...] = jnp.zeros(ptail_ref.shape, F32)

    ones2 = ones2_ref[...]
    bdmask = bdmask_ref[...]
    prow = lax.broadcasted_iota(jnp.int32, (CHUNK, ts), 0)
    plane = lax.broadcasted_iota(jnp.int32, (CHUNK, ts), 1)
    pcol = plane % CHUNK
    in_block = [plane // CHUNK == c for c in range(MIX_CHUNKS)]
    causal_p = prow >= pcol
    strict_p = prow > pcol
    eye_p = (prow == pcol).astype(F32)
    low_lanes = lax.broadcasted_iota(jnp.int32, (CHUNK, LANES), 1) < CHUNK
    t_abs = tile * ts + lax.broadcasted_iota(jnp.int32, (ts, POOL_GROUP_DIM), 0)

    def column_broadcast(row):
        return jnp.broadcast_to(row, (DN_HEAD_DIM, ts)).T

    def pack(full):
        out = full[0:CHUNK]
        for c in range(1, MIX_CHUNKS):
            out = jnp.where(in_block[c], full[c * CHUNK:(c + 1) * CHUNK], out)
        return out

    def block_diag(packed):
        return jnp.concatenate([packed.astype(BF16)] * MIX_CHUNKS, axis=0) * bdmask

    h = h_ref[...].reshape(nb * ts, D_MODEL)
    hb = h.astype(BF16)

    def project(first_col, width):
        assert width == PREP_ROWS
        return _dot(hb, win_bf[W_PIECES.index(first_col)])

    def write_strength_and_decay():
        ba_t = _dot_nt(wba_ref[...].astype(BF16), hb)
        beta = _sigmoid(ba_t)
        a_shift = ba_t + prm_ref[1]
        softplus = jnp.maximum(a_shift, 0.0) + jnp.log1p(jnp.exp(-jnp.abs(a_shift)))
        g = -jnp.exp(prm_ref[0]) * softplus
        lane_in_chunk = lax.broadcasted_iota(jnp.int32, (SUBLANES, nb * ts), 1) % CHUNK
        shift = 1
        while shift < CHUNK:
            rolled = jnp.concatenate(
                [pltpu.roll(g[:, i * LANES:(i + 1) * LANES], shift, 1) for i in range(nb * ts // LANES)],
                axis=1)
            g = g + jnp.where(lane_in_chunk >= shift, rolled, 0.0)
            shift *= 2
        return beta, g

    def gate_group(j):
        return _sigmoid(project(W_GATE0 + j * DN_WIDTH, DN_WIDTH))

    qkv = []
    gate_sig = []
    for part in range(QKV_PARTS):
        cols = slice(part * DN_WIDTH, (part + 1) * DN_WIDTH)
        raw = project(part * DN_WIDTH, DN_WIDTH)
        cw = convw_ref[:, cols]
        conv = []
        for s in seqs:
            raw_s = raw[rows_of(s)]
            ext = jnp.concatenate([xtail_ref[s, :, cols], raw_s], axis=0)
            xtail_ref[s, :, cols] = raw_s[ts - SUBLANES:, :]
            y = pltpu.roll(ext, CONV_WIDTH - 1, 0)[SUBLANES:] * cw[0:1]
            for tap in range(1, CONV_WIDTH - 1):
                y = y + pltpu.roll(ext, CONV_WIDTH - 1 - tap, 0)[SUBLANES:] * cw[tap:tap + 1]
            conv.append(y + raw_s * cw[CONV_WIDTH - 1:CONV_WIDTH])
        qkv.append(_silu(jnp.concatenate(conv, axis=0)))
        gate_sig.append(gate_group(part))
        if part == 1:
            beta_t, gr = write_strength_and_decay()
    q_all = qkv[0] * (lax.rsqrt(_head_sums(qkv[0] * qkv[0], ones2) + RMS_EPS) * (DN_HEAD_DIM ** -0.5))
    k_all = qkv[1] * lax.rsqrt(_head_sums(qkv[1] * qkv[1], ones2) + RMS_EPS)
    v_all = qkv[2]

    p = project(W_P0, POOL_WIDTH)
    pooled = []
    for s in seqs:
        p_s = p[rows_of(s)]
        tail = ptail_ref[s]
        ptail_ref[s] = p_s[ts - MAX_WINDOW:, :]
        pooled_s = []
        for gi, win in enumerate(POOL_WINDOWS):
            lo = gi * POOL_GROUP_DIM
            pg = p_s[:, lo:lo + POOL_GROUP_DIM]
            wsum = _dot(inwin_ref[gi], pg.astype(BF16))
            top = wsum[:MAX_WINDOW] + _dot(intail_ref[gi], tail[:, lo:lo + POOL_GROUP_DIM].astype(BF16))
            wsum = jnp.concatenate([top, wsum[MAX_WINDOW:]], axis=0)
            count = jnp.minimum(t_abs + 1, win).astype(F32)
            pooled_s.append(wsum / count - pg)
        pooled.append(jnp.concatenate(pooled_s, axis=1))
    pooled = jnp.concatenate(pooled, axis=0).astype(BF16)
    y_pool = _slab_dot(pooled, pproj_bf)
    z = project(W_Z0, DN_WIDTH)
    gate_sig += [gate_group(j) for j in range(QKV_PARTS, GATE_GROUPS)]

    chains = [(s, hd) for s in seqs for hd in range(DN_HEADS)]
    ids = range(len(chains))
    qn, kn_b, k_beta, v_beta, eg_b, kd_t, decay_p = [], [], [], [], [], [], []
    for s, hd in chains:
        lanes = slice(hd * DN_HEAD_DIM, (hd + 1) * DN_HEAD_DIM)
        qn.append(q_all[rows_of(s), lanes])
        kn = k_all[rows_of(s), lanes]
        v = v_all[rows_of(s), lanes]
        beta_b = column_broadcast(beta_t[hd:hd + 1, rows_of(s)])
        gr_row = gr[DN_HEADS + hd:DN_HEADS + hd + 1, rows_of(s)]
        gc_b = column_broadcast(gr_row)
        gc_p = jnp.concatenate(
            [jnp.where(low_lanes, gc_b[2 * i * CHUNK:(2 * i + 1) * CHUNK],
                       gc_b[(2 * i + 1) * CHUNK:(2 * i + 2) * CHUNK])
             for i in range(MIX_CHUNKS // 2)], axis=1)
        decay_p.append(jnp.exp(jnp.where(causal_p, gc_p - gr_row, -jnp.inf)))
        eg_b.append(jnp.exp(gc_b))
        gl_b = jnp.concatenate(
            [jnp.broadcast_to(gc_b[c * CHUNK + CHUNK - 1:(c + 1) * CHUNK, :], (CHUNK, DN_HEAD_DIM))
             for c in range(MIX_CHUNKS)], axis=0)
        k_beta.append(kn * beta_b)
        v_beta.append(v * beta_b)
        kn_b.append(kn.astype(BF16))
        kd_t.append((kn * jnp.exp(gl_b - gc_b)).T.astype(BF16))
    gram = [_dot_nt(jnp.concatenate([k_beta[i], qn[i]], axis=0).astype(BF16), kn_b[i]) for i in ids]
    lower = [pack(gram[i][:ts]) * jnp.where(strict_p, decay_p[i], 0.0) for i in ids]
    qk = [block_diag(pack(gram[i][ts:]) * decay_p[i]) for i in ids]
    t_mat = [eye_p - lower[i] for i in ids]
    power = [_dot(lower[i].astype(BF16), block_diag(lower[i])) for i in ids]
    for _ in range(NEUMANN_LEVELS - 2):
        both = [_dot(jnp.concatenate([power[i], t_mat[i]], axis=0).astype(BF16), block_diag(power[i]))
                for i in ids]
        power = [both[i][:CHUNK] for i in ids]
        t_mat = [t_mat[i] + both[i][CHUNK:] for i in ids]
    t_mat = [t_mat[i] + _dot(t_mat[i].astype(BF16), block_diag(power[i])) for i in ids]
    uw = [_dot(block_diag(t_mat[i]),
               jnp.concatenate([v_beta[i], k_beta[i] * eg_b[i]], axis=1).astype(BF16)) for i in ids]
    q_dec = [qn[i] * eg_b[i] for i in ids]
    state = [state_ref[s, hd] for s, hd in chains]
    kd_uw = []
    for i in ids:
        per_chunk = []
        for c in range(MIX_CHUNKS):
            uw_pad = jnp.concatenate(
                [uw[i][cc * CHUNK:(cc + 1) * CHUNK] if cc == c else jnp.zeros((CHUNK, 2 * DN_HEAD_DIM), F32)
                 for cc in range(MIX_CHUNKS)], axis=0).astype(BF16)
            per_chunk.append(_dot(kd_t[i], uw_pad))
        kd_uw.append(per_chunk)
    o_inter = [[] for _ in ids]
    v_new_all = [[] for _ in ids]
    for c in range(MIX_CHUNKS):
        r0 = c * CHUNK
        for i in ids:
            u_c = uw[i][r0:r0 + CHUNK, :DN_HEAD_DIM]
            w_c = uw[i][r0:r0 + CHUNK, DN_HEAD_DIM:]
            lhs = jnp.concatenate(
                [kd_uw[i][c][:, DN_HEAD_DIM:], w_c, q_dec[i][r0:r0 + CHUNK]], axis=0).astype(BF16)
            prod = _dot(lhs, state[i].astype(BF16))
            v_new_all[i].append(u_c - prod[DN_HEAD_DIM:DN_HEAD_DIM + CHUNK])
            o_inter[i].append(prod[DN_HEAD_DIM + CHUNK:])
            g_last = eg_b[i][r0 + CHUNK - 1:r0 + CHUNK, :]
            state[i] = state[i] * g_last - prod[:DN_HEAD_DIM] + kd_uw[i][c][:, :DN_HEAD_DIM]
    o_chain = []
    for i, (s, hd) in enumerate(chains):
        state_ref[s, hd] = state[i]
        v_new_full = jnp.concatenate(v_new_all[i], axis=0).astype(BF16)
        o_chain.append(jnp.concatenate(o_inter[i], axis=0) + _dot(qk[i], v_new_full))
    o_all = jnp.concatenate(
        [jnp.concatenate(o_chain[s * DN_HEADS:(s + 1) * DN_HEADS], axis=1) for s in seqs], axis=0)
    o_all = o_all * lax.rsqrt(_head_sums(o_all * o_all, ones2) * (1.0 / DN_HEAD_DIM) + RMS_EPS)
    o_all = o_all * dng_ref[...] * _silu(z)
    y_dn = _slab_dot(o_all.astype(BF16), dnproj_bf)

    half = GATE_GROUPS // 2
    merged = (jnp.concatenate(gate_sig[:half], axis=1) * y_dn
              + jnp.concatenate(gate_sig[half:], axis=1) * y_pool)
    m = _slab_dot(merged.astype(BF16), wout_bf)
    o_ref[...] = _layer_norm(ALPHA * h + m, lng_ref[...], lnb_ref[...]).reshape(nb, ts, D_MODEL)


def _mixer_ln(h3d, win_t, convw, prm, dng, dnproj, poolw, pscale, pproj, wout, lng, lnb):
    b, s, _ = h3d.shape
    assert win_t.shape == (W_GATE0 + 2 * D_MODEL, D_MODEL) and 2 * DN_HEADS == SUBLANES
    assert dnproj.shape == pproj.shape == (PREP_ROWS, D_MODEL) and wout.shape == (D_MODEL, D_MODEL)
    operands = (win_t, convw, prm, dng, dnproj, poolw, pscale, pproj, wout, lng, lnb, *_mixer_masks())
    in_hbm = (win_t, dnproj, pproj, wout)
    block = (MIX_BATCH, MIX_ROWS, D_MODEL)
    slabs = D_MODEL // SLAB_COLS
    return pl.pallas_call(
        _mixer_kernel,
        grid=(b // MIX_BATCH, s // MIX_ROWS),
        in_specs=[pl.BlockSpec(block, lambda i, j: (i, j, 0))]
        + [pl.BlockSpec(memory_space=pl.ANY) if any(op is w for w in in_hbm) else _resident(op.shape)
           for op in operands],
        out_specs=pl.BlockSpec(block, lambda i, j: (i, j, 0)),
        out_shape=jax.ShapeDtypeStruct(h3d.shape, F32),
        scratch_shapes=[
            pltpu.VMEM((MIX_BATCH, DN_HEADS, DN_HEAD_DIM, DN_HEAD_DIM), F32),
            pltpu.VMEM((MIX_BATCH, SUBLANES, QKV_COLS), F32),
            pltpu.VMEM((MIX_BATCH, MAX_WINDOW, POOL_WIDTH), F32),
            pltpu.VMEM((len(W_PIECES), D_MODEL, PREP_ROWS), BF16),
            pltpu.VMEM((SUBLANES, D_MODEL), F32),
            pltpu.VMEM((slabs, PREP_ROWS, SLAB_COLS), BF16),
            pltpu.VMEM((slabs, PREP_ROWS, SLAB_COLS), BF16),
            pltpu.VMEM((slabs, D_MODEL, SLAB_COLS), BF16),
            pltpu.VMEM((2, PREP_ROWS, D_MODEL), F32),
            pltpu.SemaphoreType.DMA((3,)),
        ],
        compiler_params=pltpu.CompilerParams(
            dimension_semantics=("arbitrary", "arbitrary"), vmem_limit_bytes=MIX_VMEM_LIMIT),
        name="mixer_ln",
    )(h3d, *operands)


def _row(v):
    return v.reshape(1, -1).astype(F32)


def kernel(x, ffn_pre_w_gate, ffn_pre_w_up, ffn_pre_w_down, norm_pre_g, norm_pre_b, mix_w_in, mix_conv_w, dn_a_log, dn_dt_bias, dn_norm_g, dn_w_proj, pool_w, pool_scale, pool_w_proj, mix_w_out, norm_mix_g, norm_mix_b, ffn_post_w_gate, ffn_post_w_up, ffn_post_w_down, norm_post_g, norm_post_b):
    bsz, seq, _ = x.shape
    h = x
    for l in range(DEPTH):
        h = _ffn_ln(h.reshape(bsz * seq, D_MODEL),
                    ffn_pre_w_gate[l], ffn_pre_w_up[l], ffn_pre_w_down[l], _row(norm_pre_g[l]), _row(norm_pre_b[l]),
                    "ffn_ln_pre")

        lanes = MIX_BATCH * MIX_ROWS
        prm = jnp.zeros((2, SUBLANES, lanes), F32)
        prm = prm.at[0, DN_HEADS:2 * DN_HEADS].set(jnp.broadcast_to(dn_a_log[l][:, None], (DN_HEADS, lanes)))
        prm = prm.at[1, DN_HEADS:2 * DN_HEADS].set(jnp.broadcast_to(dn_dt_bias[l][:, None], (DN_HEADS, lanes)))
        h = _mixer_ln(
            h.reshape(bsz, seq, D_MODEL),
            jnp.swapaxes(mix_w_in[l], 0, 1),
            mix_conv_w[l].astype(F32), prm, _row(jnp.tile(dn_norm_g[l], DN_HEADS)),
            dn_w_proj[l], pool_w[l], _row(pool_scale[l]), pool_w_proj[l], mix_w_out[l],
            _row(norm_mix_g[l]), _row(norm_mix_b[l]))

        h = _ffn_ln(h.reshape(bsz * seq, D_MODEL),
                    ffn_post_w_gate[l], ffn_post_w_up[l], ffn_post_w_down[l], _row(norm_post_g[l]), _row(norm_post_b[l]),
                    "ffn_ln_post")
    return h.reshape(bsz, seq, D_MODEL)
```

```python
import numpy as np

import jax
import jax.numpy as jnp
from jax import lax
from jax.experimental import pallas as pl
from jax.experimental.pallas import tpu as pltpu

F32 = jnp.float32
BF16 = jnp.bfloat16

D_MODEL = 1024
D_FF = 2816
DN_HEADS = 4
DN_HEAD_DIM = 128
DN_WIDTH = DN_HEADS * DN_HEAD_DIM
CONV_WIDTH = 4
CHUNK = 64
POOL_WINDOWS = (2, 4, 8, 16)
POOL_GROUPS = 4
POOL_GROUP_DIM = 128
POOL_WIDTH = POOL_GROUPS * POOL_GROUP_DIM
QKV_COLS = 3 * DN_WIDTH
W_Z0 = QKV_COLS
W_BA0 = W_Z0 + DN_WIDTH
W_P0 = W_BA0 + 2 * DN_HEADS
W_GATE0 = W_P0 + POOL_WIDTH
DEPTH = 1
ALPHA = (2.0 * DEPTH) ** 0.25
LN_EPS = 1e-5
RMS_EPS = 1e-6

SUBLANES = 8
LANES = 128
MIB = 1024 * 1024

FFN_ROWS = 512
FFN_EPILOGUE_ROWS = 64
FFN_COLS = 256
MIX_ROWS = 256
MIX_CHUNKS = MIX_ROWS // CHUNK
MIX_BATCH = 2
MAX_WINDOW = max(POOL_WINDOWS)
PREP_ROWS = 512
SLAB_COLS = 512
GATE_GROUPS = 2 * D_MODEL // DN_WIDTH
W_PIECES = ((0, DN_WIDTH, 2 * DN_WIDTH, W_Z0, W_P0)
            + tuple(W_GATE0 + j * DN_WIDTH for j in range(GATE_GROUPS)))
QKV_PARTS = QKV_COLS // DN_WIDTH
NEUMANN_LEVELS = CHUNK.bit_length() - 1
FFN_VMEM_LIMIT = 58 * MIB
MIX_VMEM_LIMIT = 56 * MIB


def _dot(a, b, **kw):
    return jnp.dot(a, b, preferred_element_type=F32, **kw)


def _dot_nt(a, b):
    return lax.dot_general(a, b, (((1,), (1,)), ((), ())), preferred_element_type=F32)


def _slab_dot(a, w_slabs):
    return jnp.concatenate([_dot(a, w_slabs[j]) for j in range(w_slabs.shape[0])], axis=1)


def _layer_norm(y, g, b):
    mu = jnp.mean(y, axis=-1, keepdims=True)
    yc = y - mu
    var = jnp.mean(yc * yc, axis=-1, keepdims=True)
    return yc * lax.rsqrt(var + LN_EPS) * g + b


def _sigmoid(x):
    return 0.5 * jnp.tanh(0.5 * x) + 0.5


def _silu(x):
    half = 0.5 * x
    return half + half * jnp.tanh(half)


def _head_sums(sq, ones2):
    sq = sq.astype(BF16)
    width = ones2.shape[0]
    return jnp.concatenate(
        [_dot(sq[:, i * width:(i + 1) * width], ones2) for i in range(sq.shape[1] // width)], axis=1)


def _issued_after(value, anchor):
    never = (anchor == anchor) & (anchor != anchor)
    return value + jnp.where(never, 1.0, 0.0)


def _ffn_ln_kernel(x_ref, x_prev_ref, wg_hbm, wu_hbm, wd_hbm, g_ref, b_ref, o_ref,
                   acc_keep, wg_bf, wu_bf, wd_bf, wg_stage, wu_stage, wd_stage, w_sem):
    i = pl.program_id(0)
    last = pl.num_programs(0) - 1
    pieces = FFN_ROWS // FFN_EPILOGUE_ROWS
    n_chunks = D_FF // FFN_COLS
    assert pieces + 2 <= n_chunks

    def weight_copies(c):
        span = pl.ds(c * FFN_COLS, FFN_COLS)
        slot = c % 2
        return (pltpu.make_async_copy(wg_hbm.at[:, span], wg_stage.at[slot], w_sem.at[0, slot]),
                pltpu.make_async_copy(wu_hbm.at[:, span], wu_stage.at[slot], w_sem.at[1, slot]),
                pltpu.make_async_copy(wd_hbm.at[span, :], wd_stage.at[slot], w_sem.at[2, slot]))

    def keep_chunk(c):
        cols = slice(c * FFN_COLS, (c + 1) * FFN_COLS)
        slot = c % 2
        wg_bf[:, cols] = wg_stage[slot].astype(BF16)
        wu_bf[:, cols] = wu_stage[slot].astype(BF16)
        down = wd_stage[slot]
        for j in range(D_MODEL // SLAB_COLS):
            wd_bf[j, cols, :] = down[:, j * SLAB_COLS:(j + 1) * SLAB_COLS].astype(BF16)

    def epilogue_piece(k, anchor=None):
        rows = slice(k * FFN_EPILOGUE_ROWS, (k + 1) * FFN_EPILOGUE_ROWS)
        x_rows = x_prev_ref[rows, :]
        if anchor is not None:
            x_rows = _issued_after(x_rows, anchor)
        y = ALPHA * x_rows + 0.5 * acc_keep[rows, :]
        o_ref[rows, :] = _layer_norm(y, g_ref[...], b_ref[...])

    def tile_matmuls(first_step):
        x = x_ref[...]
        xb = x.astype(BF16)
        acc = jnp.zeros(x.shape, F32)
        for c in range(n_chunks):
            if first_step:
                if c + 1 < n_chunks:
                    for copy in weight_copies(c + 1):
                        copy.start()
                for copy in weight_copies(c):
                    copy.wait()
                keep_chunk(c)
            cols = slice(c * FFN_COLS, (c + 1) * FFN_COLS)
            gate = _dot(xb, wg_bf[:, cols])
            up = _dot(xb, wu_bf[:, cols])
            act = (_silu(gate) * up).astype(BF16)
            acc = acc + jnp.concatenate(
                [_dot(act, wd_bf[j, cols, :]) for j in range(D_MODEL // SLAB_COLS)], axis=1)
            if 2 <= c < pieces + 2:
                epilogue_piece(c - 2, acc[0:1, :])
        acc_keep[...] = acc

    @pl.when(i == 0)
    def _():
        acc_keep[...] = jnp.zeros(acc_keep.shape, F32)
        for copy in weight_copies(0):
            copy.start()
        tile_matmuls(first_step=True)

    @pl.when((i > 0) & (i < last))
    def _():
        tile_matmuls(first_step=False)

    @pl.when(i == last)
    def _():
        for k in range(pieces):
            epilogue_piece(k)


def _resident(shape):
    return pl.BlockSpec(shape, lambda *_: (0,) * len(shape), pipeline_mode=pl.Buffered(1))


def _ffn_ln(x2d, wg, wu, wd, g, b, name):
    t = x2d.shape[0]
    tiles = t // FFN_ROWS
    return pl.pallas_call(
        _ffn_ln_kernel,
        grid=(tiles + 1,),
        in_specs=[
            pl.BlockSpec((FFN_ROWS, D_MODEL), lambda i: (jnp.minimum(i, tiles - 1), 0)),
            pl.BlockSpec((FFN_ROWS, D_MODEL), lambda i: (jnp.maximum(i - 1, 0), 0)),
            pl.BlockSpec(memory_space=pl.ANY),
            pl.BlockSpec(memory_space=pl.ANY),
            pl.BlockSpec(memory_space=pl.ANY),
            _resident((1, D_MODEL)),
            _resident((1, D_MODEL)),
        ],
        out_specs=pl.BlockSpec((FFN_ROWS, D_MODEL), lambda i: (jnp.maximum(i - 1, 0), 0)),
        out_shape=jax.ShapeDtypeStruct((t, D_MODEL), F32),
        scratch_shapes=[
            pltpu.VMEM((FFN_ROWS, D_MODEL), F32),
            pltpu.VMEM(wg.shape, BF16),
            pltpu.VMEM(wu.shape, BF16),
            pltpu.VMEM((D_MODEL // SLAB_COLS, D_FF, SLAB_COLS), BF16),
            pltpu.VMEM((2, D_MODEL, FFN_COLS), F32),
            pltpu.VMEM((2, D_MODEL, FFN_COLS), F32),
            pltpu.VMEM((2, FFN_COLS, D_MODEL), F32),
            pltpu.SemaphoreType.DMA((3, 2)),
        ],
        compiler_params=pltpu.CompilerParams(
            dimension_semantics=("arbitrary",), vmem_limit_bytes=FFN_VMEM_LIMIT),
        name=name,
    )(x2d, x2d, wg, wu, wd, g, b)


def _mixer_masks():
    r = np.arange(MIX_ROWS)[:, None]
    c = np.arange(MIX_ROWS)[None, :]
    block_diag = (r // CHUNK == c // CHUNK)
    in_win = np.stack([(c <= r) & (c > r - w) for w in POOL_WINDOWS])
    rt = np.arange(MAX_WINDOW)[:, None]
    ct = np.arange(MAX_WINDOW)[None, :]
    in_tail = np.stack([(ct - MAX_WINDOW > rt - w) for w in POOL_WINDOWS])
    head_pair_ones = (r // DN_HEAD_DIM == c // DN_HEAD_DIM)
    as_bf16 = lambda m: jnp.asarray(m.astype(np.float32), dtype=BF16)
    return as_bf16(block_diag), as_bf16(in_win), as_bf16(in_tail), as_bf16(head_pair_ones)


def _mixer_kernel(h_ref, win_hbm, convw_ref, prm_ref, dng_ref,
                  dnproj_hbm, poolw_ref, pscale_ref, pproj_hbm, wout_hbm, lng_ref, lnb_ref,
                  bdmask_ref, inwin_ref, intail_ref, ones2_ref,
                  o_ref, state_ref, xtail_ref, ptail_ref,
                  win_bf, wba_ref, dnproj_bf, pproj_bf, wout_bf, stage, prep_sem):
    ts = MIX_ROWS
    nb = MIX_BATCH
    tile = pl.program_id(1)
    seqs = range(nb)
    rows_of = lambda s: slice(s * ts, (s + 1) * ts)

    @pl.when((pl.program_id(0) == 0) & (tile == 0))
    def _():
        def store_transposed(k):
            def store(x):
                win_bf[k] = x.T.astype(BF16)
            return store

        def store_slabs(dst, row_block):
            def store(x):
                rows = slice(row_block * PREP_ROWS, (row_block + 1) * PREP_ROWS)
                for j in range(D_MODEL // SLAB_COLS):
                    dst[j, rows, :] = x[:, j * SLAB_COLS:(j + 1) * SLAB_COLS].astype(BF16)
            return store

        jobs = [(win_hbm.at[pl.ds(first, PREP_ROWS), :], store_transposed(k))
                for k, first in enumerate(W_PIECES)]
        jobs.append((dnproj_hbm, store_slabs(dnproj_bf, 0)))

        def store_pool_projection(x):
            folded = []
            for g in range(POOL_GROUPS):
                rows = slice(g * POOL_GROUP_DIM, (g + 1) * POOL_GROUP_DIM)
                folded.append(_dot(poolw_ref[g] * pscale_ref[:, rows], x[rows, :],
                                   precision=lax.Precision.HIGHEST))
            store_slabs(pproj_bf, 0)(jnp.concatenate(folded, axis=0))

        jobs.append((pproj_hbm, store_pool_projection))
        jobs += [(wout_hbm.at[pl.ds(r * PREP_ROWS, PREP_ROWS), :], store_slabs(wout_bf, r))
                 for r in range(D_MODEL // PREP_ROWS)]
        ba_copy = pltpu.make_async_copy(win_hbm.at[pl.ds(W_BA0, SUBLANES), :], wba_ref, prep_sem.at[2])
        ba_copy.start()
        copies = [pltpu.make_async_copy(src, stage.at[n % 2], prep_sem.at[n % 2])
                  for n, (src, _) in enumerate(jobs)]
        copies[0].start()
        for n, (_, store) in enumerate(jobs):
            if n + 1 < len(jobs):
                copies[n + 1].start()
            copies[n].wait()
            store(stage[n % 2])
        ba_copy.wait()

    @pl.when(tile == 0)
    def _():
        state_ref[...] = jnp.zeros(state_ref.shape, F32)
        xtail_ref[...] = jnp.zeros(xtail_ref.shape, F32)
        ptail_ref[...] = jnp.zeros(ptail_ref.shape, F32)

    ones2 = ones2_ref[...]
    bdmask = bdmask_ref[...]
    prow = lax.broadcasted_iota(jnp.int32, (CHUNK, ts), 0)
    plane = lax.broadcasted_iota(jnp.int32, (CHUNK, ts), 1)
    pcol = plane % CHUNK
    in_block = [plane // CHUNK == c for c in range(MIX_CHUNKS)]
    causal_p = prow >= pcol
    strict_p = prow > pcol
    eye_p = (prow == pcol).astype(F32)
    low_lanes = lax.broadcasted_iota(jnp.int32, (CHUNK, LANES), 1) < CHUNK
    t_abs = tile * ts + lax.broadcasted_iota(jnp.int32, (ts, POOL_GROUP_DIM), 0)

    def column_broadcast(row):
        return jnp.broadcast_to(row, (DN_HEAD_DIM, ts)).T

    def pack(full):
        out = full[0:CHUNK]
        for c in range(1, MIX_CHUNKS):
            out = jnp.where(in_block[c], full[c * CHUNK:(c + 1) * CHUNK], out)
        return out

    def block_diag(packed):
        return jnp.concatenate([packed.astype(BF16)] * MIX_CHUNKS, axis=0) * bdmask

    h = h_ref[...].reshape(nb * ts, D_MODEL)
    hb = h.astype(BF16)

    def project(first_col, width):
        assert width == PREP_ROWS
        return _dot(hb, win_bf[W_PIECES.index(first_col)])

    def write_strength_and_decay():
        ba_t = _dot_nt(wba_ref[...].astype(BF16), hb)
        beta = _sigmoid(ba_t)
        a_shift = ba_t + prm_ref[1]
        softplus = jnp.maximum(a_shift, 0.0) + jnp.log1p(jnp.exp(-jnp.abs(a_shift)))
        g = -jnp.exp(prm_ref[0]) * softplus
        lane_in_chunk = lax.broadcasted_iota(jnp.int32, (SUBLANES, nb * ts), 1) % CHUNK
        shift = 1
        while shift < CHUNK:
            rolled = jnp.concatenate(
                [pltpu.roll(g[:, i * LANES:(i + 1) * LANES], shift, 1) for i in range(nb * ts // LANES)],
                axis=1)
            g = g + jnp.where(lane_in_chunk >= shift, rolled, 0.0)
            shift *= 2
        return beta, g

    def gate_group(j):
        return _sigmoid(project(W_GATE0 + j * DN_WIDTH, DN_WIDTH))

    qkv = []
    gate_sig = []
    for part in range(QKV_PARTS):
        cols = slice(part * DN_WIDTH, (part + 1) * DN_WIDTH)
        raw = project(part * DN_WIDTH, DN_WIDTH)
        cw = convw_ref[:, cols]
        conv = []
        for s in seqs:
            heads_out = []
            for hd in range(DN_HEADS):
                lanes = slice(hd * DN_HEAD_DIM, (hd + 1) * DN_HEAD_DIM)
                ref_lanes = slice(part * DN_WIDTH + hd * DN_HEAD_DIM, part * DN_WIDTH + (hd + 1) * DN_HEAD_DIM)
                raw_h = raw[rows_of(s), lanes]
                ext = jnp.concatenate([xtail_ref[s, :, ref_lanes], raw_h], axis=0)
                xtail_ref[s, :, ref_lanes] = raw_h[ts - SUBLANES:, :]
                y = pltpu.roll(ext, CONV_WIDTH - 1, 0)[SUBLANES:] * cw[0:1, lanes]
                for tap in range(1, CONV_WIDTH - 1):
                    y = y + pltpu.roll(ext, CONV_WIDTH - 1 - tap, 0)[SUBLANES:] * cw[tap:tap + 1, lanes]
                heads_out.append(_silu(y + raw_h * cw[CONV_WIDTH - 1:CONV_WIDTH, lanes]))
            conv.append(jnp.concatenate(heads_out, axis=1))
        qkv.append(jnp.concatenate(conv, axis=0))
        gate_sig.append(gate_group(part))
        if part == 1:
            beta_t, gr = write_strength_and_decay()
    q_all = qkv[0] * (lax.rsqrt(_head_sums(qkv[0] * qkv[0], ones2) + RMS_EPS) * (DN_HEAD_DIM ** -0.5))
    k_all = qkv[1] * lax.rsqrt(_head_sums(qkv[1] * qkv[1], ones2) + RMS_EPS)
    v_all = qkv[2]

    p = project(W_P0, POOL_WIDTH)
    pooled = []
    for s in seqs:
        p_s = p[rows_of(s)]
        tail = ptail_ref[s]
        ptail_ref[s] = p_s[ts - MAX_WINDOW:, :]
        pooled_s = []
        for gi, win in enumerate(POOL_WINDOWS):
            lo = gi * POOL_GROUP_DIM
            pg = p_s[:, lo:lo + POOL_GROUP_DIM]
            wsum = _dot(inwin_ref[gi], pg.astype(BF16))
            top = wsum[:MAX_WINDOW] + _dot(intail_ref[gi], tail[:, lo:lo + POOL_GROUP_DIM].astype(BF16))
            wsum = jnp.concatenate([top, wsum[MAX_WINDOW:]], axis=0)
            count = jnp.minimum(t_abs + 1, win).astype(F32)
            pooled_s.append(wsum / count - pg)
        pooled.append(jnp.concatenate(pooled_s, axis=1))
    pooled = jnp.concatenate(pooled, axis=0).astype(BF16)
    y_pool = _slab_dot(pooled, pproj_bf)
    z = project(W_Z0, DN_WIDTH)
    gate_sig += [gate_group(j) for j in range(QKV_PARTS, GATE_GROUPS)]

    chains = [(s, hd) for s in seqs for hd in range(DN_HEADS)]
    ids = range(len(chains))
    qn, kn_b, k_beta, v_beta, eg_b, kd_t, decay_p = [], [], [], [], [], [], []
    for s, hd in chains:
        lanes = slice(hd * DN_HEAD_DIM, (hd + 1) * DN_HEAD_DIM)
        qn.append(q_all[rows_of(s), lanes])
        kn = k_all[rows_of(s), lanes]
        v = v_all[rows_of(s), lanes]
        beta_b = column_broadcast(beta_t[hd:hd + 1, rows_of(s)])
        gr_row = gr[DN_HEADS + hd:DN_HEADS + hd + 1, rows_of(s)]
        gc_b = column_broadcast(gr_row)
        gc_p = jnp.concatenate(
            [jnp.where(low_lanes, gc_b[2 * i * CHUNK:(2 * i + 1) * CHUNK],
                       gc_b[(2 * i + 1) * CHUNK:(2 * i + 2) * CHUNK])
             for i in range(MIX_CHUNKS // 2)], axis=1)
        decay_p.append(jnp.exp(jnp.where(causal_p, gc_p - gr_row, -jnp.inf)))
        eg_b.append(jnp.exp(gc_b))
        gl_b = jnp.concatenate(
            [jnp.broadcast_to(gc_b[c * CHUNK + CHUNK - 1:(c + 1) * CHUNK, :], (CHUNK, DN_HEAD_DIM))
             for c in range(MIX_CHUNKS)], axis=0)
        k_beta.append(kn * beta_b)
        v_beta.append(v * beta_b)
        kn_b.append(kn.astype(BF16))
        kd_t.append((kn * jnp.exp(gl_b - gc_b)).T.astype(BF16))
    gram = [_dot_nt(jnp.concatenate([k_beta[i], qn[i]], axis=0).astype(BF16), kn_b[i]) for i in ids]
    lower = [pack(gram[i][:ts]) * jnp.where(strict_p, decay_p[i], 0.0) for i in ids]
    qk = [block_diag(pack(gram[i][ts:]) * decay_p[i]) for i in ids]
    t_mat = [eye_p - lower[i] for i in ids]
    power = [_dot(lower[i].astype(BF16), block_diag(lower[i])) for i in ids]
    for _ in range(NEUMANN_LEVELS - 2):
        both = [_dot(jnp.concatenate([power[i], t_mat[i]], axis=0).astype(BF16), block_diag(power[i]))
                for i in ids]
        power = [both[i][:CHUNK] for i in ids]
        t_mat = [t_mat[i] + both[i][CHUNK:] for i in ids]
    t_mat = [t_mat[i] + _dot(t_mat[i].astype(BF16), block_diag(power[i])) for i in ids]
    uw = [_dot(block_diag(t_mat[i]),
               jnp.concatenate([v_beta[i], k_beta[i] * eg_b[i]], axis=1).astype(BF16)) for i in ids]
    q_dec = [qn[i] * eg_b[i] for i in ids]
    state = [state_ref[s, hd] for s, hd in chains]
    kd_uw = []
    for i in ids:
        per_chunk = []
        for c in range(MIX_CHUNKS):
            uw_pad = jnp.concatenate(
                [uw[i][cc * CHUNK:(cc + 1) * CHUNK] if cc == c else jnp.zeros((CHUNK, 2 * DN_HEAD_DIM), F32)
                 for cc in range(MIX_CHUNKS)], axis=0).astype(BF16)
            per_chunk.append(_dot(kd_t[i], uw_pad))
        kd_uw.append(per_chunk)
    o_inter = [[] for _ in ids]
    v_new_all = [[] for _ in ids]
    for c in range(MIX_CHUNKS):
        r0 = c * CHUNK
        for i in ids:
            u_c = uw[i][r0:r0 + CHUNK, :DN_HEAD_DIM]
            w_c = uw[i][r0:r0 + CHUNK, DN_HEAD_DIM:]
            lhs = jnp.concatenate(
                [kd_uw[i][c][:, DN_HEAD_DIM:], w_c, q_dec[i][r0:r0 + CHUNK]], axis=0).astype(BF16)
            prod = _dot(lhs, state[i].astype(BF16))
            v_new_all[i].append(u_c - prod[DN_HEAD_DIM:DN_HEAD_DIM + CHUNK])
            o_inter[i].append(prod[DN_HEAD_DIM + CHUNK:])
            g_last = eg_b[i][r0 + CHUNK - 1:r0 + CHUNK, :]
            state[i] = state[i] * g_last - prod[:DN_HEAD_DIM] + kd_uw[i][c][:, :DN_HEAD_DIM]
    o_chain = []
    for i, (s, hd) in enumerate(chains):
        state_ref[s, hd] = state[i]
        v_new_full = jnp.concatenate(v_new_all[i], axis=0).astype(BF16)
        o_chain.append(jnp.concatenate(o_inter[i], axis=0) + _dot(qk[i], v_new_full))
    o_all = jnp.concatenate(
        [jnp.concatenate(o_chain[s * DN_HEADS:(s + 1) * DN_HEADS], axis=1) for s in seqs], axis=0)
    o_all = o_all * lax.rsqrt(_head_sums(o_all * o_all, ones2) * (1.0 / DN_HEAD_DIM) + RMS_EPS)
    o_all = o_all * dng_ref[...] * _silu(z)
    y_dn = _slab_dot(o_all.astype(BF16), dnproj_bf)

    half = GATE_GROUPS // 2
    merged = (jnp.concatenate(gate_sig[:half], axis=1) * y_dn
              + jnp.concatenate(gate_sig[half:], axis=1) * y_pool)
    m = _slab_dot(merged.astype(BF16), wout_bf)
    o_ref[...] = _layer_norm(ALPHA * h + m, lng_ref[...], lnb_ref[...]).reshape(nb, ts, D_MODEL)


def _mixer_ln(h3d, win_t, convw, prm, dng, dnproj, poolw, pscale, pproj, wout, lng, lnb):
    b, s, _ = h3d.shape
    assert win_t.shape == (W_GATE0 + 2 * D_MODEL, D_MODEL) and 2 * DN_HEADS == SUBLANES
    assert dnproj.shape == pproj.shape == (PREP_ROWS, D_MODEL) and wout.shape == (D_MODEL, D_MODEL)
    operands = (win_t, convw, prm, dng, dnproj, poolw, pscale, pproj, wout, lng, lnb, *_mixer_masks())
    in_hbm = (win_t, dnproj, pproj, wout)
    block = (MIX_BATCH, MIX_ROWS, D_MODEL)
    slabs = D_MODEL // SLAB_COLS
    return pl.pallas_call(
        _mixer_kernel,
        grid=(b // MIX_BATCH, s // MIX_ROWS),
        in_specs=[pl.BlockSpec(block, lambda i, j: (i, j, 0))]
        + [pl.BlockSpec(memory_space=pl.ANY) if any(op is w for w in in_hbm) else _resident(op.shape)
           for op in operands],
        out_specs=pl.BlockSpec(block, lambda i, j: (i, j, 0)),
        out_shape=jax.ShapeDtypeStruct(h3d.shape, F32),
        scratch_shapes=[
            pltpu.VMEM((MIX_BATCH, DN_HEADS, DN_HEAD_DIM, DN_HEAD_DIM), F32),
            pltpu.VMEM((MIX_BATCH, SUBLANES, QKV_COLS), F32),
            pltpu.VMEM((MIX_BATCH, MAX_WINDOW, POOL_WIDTH), F32),
            pltpu.VMEM((len(W_PIECES), D_MODEL, PREP_ROWS), BF16),
            pltpu.VMEM((SUBLANES, D_MODEL), F32),
            pltpu.VMEM((slabs, PREP_ROWS, SLAB_COLS), BF16),
            pltpu.VMEM((slabs, PREP_ROWS, SLAB_COLS), BF16),
            pltpu.VMEM((slabs, D_MODEL, SLAB_COLS), BF16),
            pltpu.VMEM((2, PREP_ROWS, D_MODEL), F32),
            pltpu.SemaphoreType.DMA((3,)),
        ],
        compiler_params=pltpu.CompilerParams(
            dimension_semantics=("arbitrary", "arbitrary"), vmem_limit_bytes=MIX_VMEM_LIMIT),
        name="mixer_ln",
    )(h3d, *operands)


def _row(v):
    return v.reshape(1, -1).astype(F32)


def kernel(x, ffn_pre_w_gate, ffn_pre_w_up, ffn_pre_w_down, norm_pre_g, norm_pre_b, mix_w_in, mix_conv_w, dn_a_log, dn_dt_bias, dn_norm_g, dn_w_proj, pool_w, pool_scale, pool_w_proj, mix_w_out, norm_mix_g, norm_mix_b, ffn_post_w_gate, ffn_post_w_up, ffn_post_w_down, norm_post_g, norm_post_b):
    bsz, seq, _ = x.shape
    h = x
    for l in range(DEPTH):
        h = _ffn_ln(h.reshape(bsz * seq, D_MODEL),
                    ffn_pre_w_gate[l], ffn_pre_w_up[l], ffn_pre_w_down[l], _row(norm_pre_g[l]), _row(norm_pre_b[l]),
                    "ffn_ln_pre")

        lanes = MIX_BATCH * MIX_ROWS
        prm = jnp.zeros((2, SUBLANES, lanes), F32)
        prm = prm.at[0, DN_HEADS:2 * DN_HEADS].set(jnp.broadcast_to(dn_a_log[l][:, None], (DN_HEADS, lanes)))
        prm = prm.at[1, DN_HEADS:2 * DN_HEADS].set(jnp.broadcast_to(dn_dt_bias[l][:, None], (DN_HEADS, lanes)))
        h = _mixer_ln(
            h.reshape(bsz, seq, D_MODEL),
            jnp.swapaxes(mix_w_in[l], 0, 1),
            mix_conv_w[l].astype(F32), prm, _row(jnp.tile(dn_norm_g[l], DN_HEADS)),
            dn_w_proj[l], pool_w[l], _row(pool_scale[l]), pool_w_proj[l], mix_w_out[l],
            _row(norm_mix_g[l]), _row(norm_mix_b[l]))

        h = _ffn_ln(h.reshape(bsz * seq, D_MODEL),
                    ffn_post_w_gate[l], ffn_post_w_up[l], ffn_post_w_down[l], _row(norm_post_g[l]), _row(norm_post_b[l]),
                    "ffn_ln_post")
    return h.reshape(bsz, seq, D_MODEL)
```

```python
import numpy as np

import jax
import jax.numpy as jnp
from jax import lax
from jax.experimental import pallas as pl
from jax.experimental.pallas import tpu as pltpu

F32 = jnp.float32
BF16 = jnp.bfloat16

D_MODEL = 1024
D_FF = 2816
DN_HEADS = 4
DN_HEAD_DIM = 128
DN_WIDTH = DN_HEADS * DN_HEAD_DIM
CONV_WIDTH = 4
CHUNK = 64
POOL_WINDOWS = (2, 4, 8, 16)
POOL_GROUPS = 4
POOL_GROUP_DIM = 128
POOL_WIDTH = POOL_GROUPS * POOL_GROUP_DIM
QKV_COLS = 3 * DN_WIDTH
W_Z0 = QKV_COLS
W_BA0 = W_Z0 + DN_WIDTH
W_P0 = W_BA0 + 2 * DN_HEADS
W_GATE0 = W_P0 + POOL_WIDTH
DEPTH = 1
ALPHA = (2.0 * DEPTH) ** 0.25
LN_EPS = 1e-5
RMS_EPS = 1e-6

SUBLANES = 8
LANES = 128
MIB = 1024 * 1024

FFN_ROWS = 512
FFN_EPILOGUE_ROWS = 64
FFN_COLS = 256
MIX_ROWS = 256
MIX_CHUNKS = MIX_ROWS // CHUNK
MIX_BATCH = 2
MAX_WINDOW = max(POOL_WINDOWS)
PREP_ROWS = 512
SLAB_COLS = 512
GATE_GROUPS = 2 * D_MODEL // DN_WIDTH
W_PIECES = ((0, DN_WIDTH, 2 * DN_WIDTH, W_Z0, W_P0)
            + tuple(W_GATE0 + j * DN_WIDTH for j in range(GATE_GROUPS)))
QKV_PARTS = QKV_COLS // DN_WIDTH
NEUMANN_LEVELS = CHUNK.bit_length() - 1
FFN_VMEM_LIMIT = 58 * MIB
MIX_VMEM_LIMIT = 56 * MIB


def _dot(a, b, **kw):
    return jnp.dot(a, b, preferred_element_type=F32, **kw)


def _dot_nt(a, b):
    return lax.dot_general(a, b, (((1,), (1,)), ((), ())), preferred_element_type=F32)


def _slab_dot(a, w_slabs):
    return jnp.concatenate([_dot(a, w_slabs[j]) for j in range(w_slabs.shape[0])], axis=1)


def _layer_norm(y, g, b):
    mu = jnp.mean(y, axis=-1, keepdims=True)
    yc = y - mu
    var = jnp.mean(yc * yc, axis=-1, keepdims=True)
    return yc * lax.rsqrt(var + LN_EPS) * g + b


def _sigmoid(x):
    return 0.5 * jnp.tanh(0.5 * x) + 0.5


def _silu(x):
    half = 0.5 * x
    return half + half * jnp.tanh(half)


def _head_sums(sq, ones2):
    sq = sq.astype(BF16)
    width = ones2.shape[0]
    return jnp.concatenate(
        [_dot(sq[:, i * width:(i + 1) * width], ones2) for i in range(sq.shape[1] // width)], axis=1)


def _issued_after(value, anchor):
    never = (anchor == anchor) & (anchor != anchor)
    return value + jnp.where(never, 1.0, 0.0)


def _ffn_ln_kernel(x_ref, x_prev_ref, wg_hbm, wu_hbm, wd_hbm, g_ref, b_ref, o_ref,
                   acc_keep, wg_bf, wu_bf, wd_bf, wg_stage, wu_stage, wd_stage, w_sem):
    i = pl.program_id(0)
    last = pl.num_programs(0) - 1
    pieces = FFN_ROWS // FFN_EPILOGUE_ROWS
    n_chunks = D_FF // FFN_COLS
    assert pieces + 2 <= n_chunks

    def weight_copies(c):
        span = pl.ds(c * FFN_COLS, FFN_COLS)
        slot = c % 2
        return (pltpu.make_async_copy(wg_hbm.at[:, span], wg_stage.at[slot], w_sem.at[0, slot]),
                pltpu.make_async_copy(wu_hbm.at[:, span], wu_stage.at[slot], w_sem.at[1, slot]),
                pltpu.make_async_copy(wd_hbm.at[span, :], wd_stage.at[slot], w_sem.at[2, slot]))

    def keep_chunk(c):
        cols = slice(c * FFN_COLS, (c + 1) * FFN_COLS)
        slot = c % 2
        wg_bf[:, cols] = wg_stage[slot].astype(BF16)
        wu_bf[:, cols] = wu_stage[slot].astype(BF16)
        down = wd_stage[slot]
        for j in range(D_MODEL // SLAB_COLS):
            wd_bf[j, cols, :] = down[:, j * SLAB_COLS:(j + 1) * SLAB_COLS].astype(BF16)

    def epilogue_piece(k, anchor=None):
        rows = slice(k * FFN_EPILOGUE_ROWS, (k + 1) * FFN_EPILOGUE_ROWS)
        x_rows = x_prev_ref[rows, :]
        if anchor is not None:
            x_rows = _issued_after(x_rows, anchor)
        y = ALPHA * x_rows + 0.5 * acc_keep[rows, :]
        o_ref[rows, :] = _layer_norm(y, g_ref[...], b_ref[...])

    def tile_matmuls(first_step):
        x = x_ref[...]
        xb = x.astype(BF16)
        acc = jnp.zeros(x.shape, F32)
        for c in range(n_chunks):
            if first_step:
                if c + 1 < n_chunks:
                    for n, copy in enumerate(weight_copies(c + 1)):
                        copy.start(priority=n % 2)
                for copy in weight_copies(c):
                    copy.wait()
                keep_chunk(c)
            cols = slice(c * FFN_COLS, (c + 1) * FFN_COLS)
            gate = _dot(xb, wg_bf[:, cols])
            up = _dot(xb, wu_bf[:, cols])
            act = (_silu(gate) * up).astype(BF16)
            acc = acc + jnp.concatenate(
                [_dot(act, wd_bf[j, cols, :]) for j in range(D_MODEL // SLAB_COLS)], axis=1)
            if 2 <= c < pieces + 2:
                epilogue_piece(c - 2, acc[0:1, :])
        acc_keep[...] = acc

    @pl.when(i == 0)
    def _():
        acc_keep[...] = jnp.zeros(acc_keep.shape, F32)
        for n, copy in enumerate(weight_copies(0)):
            copy.start(priority=n % 2)
        tile_matmuls(first_step=True)

    @pl.when((i > 0) & (i < last))
    def _():
        tile_matmuls(first_step=False)

    @pl.when(i == last)
    def _():
        for k in range(pieces):
            epilogue_piece(k)


def _resident(shape):
    return pl.BlockSpec(shape, lambda *_: (0,) * len(shape), pipeline_mode=pl.Buffered(1))


def _ffn_ln(x2d, wg, wu, wd, g, b, name):
    t = x2d.shape[0]
    tiles = t // FFN_ROWS
    return pl.pallas_call(
        _ffn_ln_kernel,
        grid=(tiles + 1,),
        in_specs=[
            pl.BlockSpec((FFN_ROWS, D_MODEL), lambda i: (jnp.minimum(i, tiles - 1), 0)),
            pl.BlockSpec((FFN_ROWS, D_MODEL), lambda i: (jnp.maximum(i - 1, 0), 0)),
            pl.BlockSpec(memory_space=pl.ANY),
            pl.BlockSpec(memory_space=pl.ANY),
            pl.BlockSpec(memory_space=pl.ANY),
            _resident((1, D_MODEL)),
            _resident((1, D_MODEL)),
        ],
        out_specs=pl.BlockSpec((FFN_ROWS, D_MODEL), lambda i: (jnp.maximum(i - 1, 0), 0)),
        out_shape=jax.ShapeDtypeStruct((t, D_MODEL), F32),
        scratch_shapes=[
            pltpu.VMEM((FFN_ROWS, D_MODEL), F32),
            pltpu.VMEM(wg.shape, BF16),
            pltpu.VMEM(wu.shape, BF16),
            pltpu.VMEM((D_MODEL // SLAB_COLS, D_FF, SLAB_COLS), BF16),
            pltpu.VMEM((2, D_MODEL, FFN_COLS), F32),
            pltpu.VMEM((2, D_MODEL, FFN_COLS), F32),
            pltpu.VMEM((2, FFN_COLS, D_MODEL), F32),
            pltpu.SemaphoreType.DMA((3, 2)),
        ],
        compiler_params=pltpu.CompilerParams(
            dimension_semantics=("arbitrary",), vmem_limit_bytes=FFN_VMEM_LIMIT),
        name=name,
    )(x2d, x2d, wg, wu, wd, g, b)


def _mixer_masks():
    r = np.arange(MIX_ROWS)[:, None]
    c = np.arange(MIX_ROWS)[None, :]
    block_diag = (r // CHUNK == c // CHUNK)
    in_win = np.stack([(c <= r) & (c > r - w) for w in POOL_WINDOWS])
    rt = np.arange(MAX_WINDOW)[:, None]
    ct = np.arange(MAX_WINDOW)[None, :]
    in_tail = np.stack([(ct - MAX_WINDOW > rt - w) for w in POOL_WINDOWS])
    head_pair_ones = (r // DN_HEAD_DIM == c // DN_HEAD_DIM)
    as_bf16 = lambda m: jnp.asarray(m.astype(np.float32), dtype=BF16)
    return as_bf16(block_diag), as_bf16(in_win), as_bf16(in_tail), as_bf16(head_pair_ones)


def _mixer_kernel(h_ref, win_hbm, convw_ref, prm_ref, dng_ref,
                  dnproj_hbm, poolw_ref, pscale_ref, pproj_hbm, wout_hbm, lng_ref, lnb_ref,
                  bdmask_ref, inwin_ref, intail_ref, ones2_ref,
                  o_ref, state_ref, xtail_ref, ptail_ref,
                  win_bf, wba_ref, dnproj_bf, pproj_bf, wout_bf, stage, prep_sem):
    ts = MIX_ROWS
    nb = MIX_BATCH
    tile = pl.program_id(1)
    seqs = range(nb)
    rows_of = lambda s: slice(s * ts, (s + 1) * ts)

    @pl.when((pl.program_id(0) == 0) & (tile == 0))
    def _():
        def store_transposed(k):
            def store(x):
                win_bf[k] = x.T.astype(BF16)
            return store

        def store_slabs(dst, row_block):
            def store(x):
                rows = slice(row_block * PREP_ROWS, (row_block + 1) * PREP_ROWS)
                for j in range(D_MODEL // SLAB_COLS):
                    dst[j, rows, :] = x[:, j * SLAB_COLS:(j + 1) * SLAB_COLS].astype(BF16)
            return store

        jobs = [(win_hbm.at[pl.ds(first, PREP_ROWS), :], store_transposed(k))
                for k, first in enumerate(W_PIECES)]
        jobs.append((dnproj_hbm, store_slabs(dnproj_bf, 0)))

        def store_pool_projection(x):
            folded = []
            for g in range(POOL_GROUPS):
                rows = slice(g * POOL_GROUP_DIM, (g + 1) * POOL_GROUP_DIM)
                folded.append(_dot(poolw_ref[g] * pscale_ref[:, rows], x[rows, :],
                                   precision=lax.Precision.HIGHEST))
            store_slabs(pproj_bf, 0)(jnp.concatenate(folded, axis=0))

        jobs.append((pproj_hbm, store_pool_projection))
        jobs += [(wout_hbm.at[pl.ds(r * PREP_ROWS, PREP_ROWS), :], store_slabs(wout_bf, r))
                 for r in range(D_MODEL // PREP_ROWS)]
        ba_copy = pltpu.make_async_copy(win_hbm.at[pl.ds(W_BA0, SUBLANES), :], wba_ref, prep_sem.at[2])
        ba_copy.start()
        copies = [pltpu.make_async_copy(src, stage.at[n % 2], prep_sem.at[n % 2])
                  for n, (src, _) in enumerate(jobs)]
        copies[0].start()
        for n, (_, store) in enumerate(jobs):
            if n + 1 < len(jobs):
                copies[n + 1].start()
            copies[n].wait()
            store(stage[n % 2])
        ba_copy.wait()

    @pl.when(tile == 0)
    def _():
        state_ref[...] = jnp.zeros(state_ref.shape, F32)
        xtail_ref[...] = jnp.zeros(xtail_ref.shape, F32)
        ptail_ref[...] = jnp.zeros(ptail_ref.shape, F32)

    ones2 = ones2_ref[...]
    bdmask = bdmask_ref[...]
    prow = lax.broadcasted_iota(jnp.int32, (CHUNK, ts), 0)
    plane = lax.broadcasted_iota(jnp.int32, (CHUNK, ts), 1)
    pcol = plane % CHUNK
    in_block = [plane // CHUNK == c for c in range(MIX_CHUNKS)]
    causal_p = prow >= pcol
    strict_p = prow > pcol
    eye_p = (prow == pcol).astype(F32)
    low_lanes = lax.broadcasted_iota(jnp.int32, (CHUNK, LANES), 1) < CHUNK
    t_abs = tile * ts + lax.broadcasted_iota(jnp.int32, (ts, POOL_GROUP_DIM), 0)

    def column_broadcast(row):
        return jnp.broadcast_to(row, (DN_HEAD_DIM, ts)).T

    def pack(full):
        out = full[0:CHUNK]
        for c in range(1, MIX_CHUNKS):
            out = jnp.where(in_block[c], full[c * CHUNK:(c + 1) * CHUNK], out)
        return out

    def block_diag(packed):
        return jnp.concatenate([packed.astype(BF16)] * MIX_CHUNKS, axis=0) * bdmask

    h = h_ref[...].reshape(nb * ts, D_MODEL)
    hb = h.astype(BF16)

    def project(first_col, width):
        assert width == PREP_ROWS
        return _dot(hb, win_bf[W_PIECES.index(first_col)])

    def write_strength_and_decay():
        ba_t = _dot_nt(wba_ref[...].astype(BF16), hb)
        beta = _sigmoid(ba_t)
        a_shift = ba_t + prm_ref[1]
        softplus = jnp.maximum(a_shift, 0.0) + jnp.log1p(jnp.exp(-jnp.abs(a_shift)))
        g = -jnp.exp(prm_ref[0]) * softplus
        lane_in_chunk = lax.broadcasted_iota(jnp.int32, (SUBLANES, nb * ts), 1) % CHUNK
        shift = 1
        while shift < CHUNK:
            rolled = jnp.concatenate(
                [pltpu.roll(g[:, i * LANES:(i + 1) * LANES], shift, 1) for i in range(nb * ts // LANES)],
                axis=1)
            g = g + jnp.where(lane_in_chunk >= shift, rolled, 0.0)
            shift *= 2
        return beta, g

    def gate_group(j):
        return _sigmoid(project(W_GATE0 + j * DN_WIDTH, DN_WIDTH))

    qkv = []
    gate_sig = []
    for part in range(QKV_PARTS):
        cols = slice(part * DN_WIDTH, (part + 1) * DN_WIDTH)
        raw = project(part * DN_WIDTH, DN_WIDTH)
        cw = convw_ref[:, cols]
        conv = []
        for s in seqs:
            heads_out = []
            for hd in range(DN_HEADS):
                lanes = slice(hd * DN_HEAD_DIM, (hd + 1) * DN_HEAD_DIM)
                ref_lanes = slice(part * DN_WIDTH + hd * DN_HEAD_DIM, part * DN_WIDTH + (hd + 1) * DN_HEAD_DIM)
                raw_h = raw[rows_of(s), lanes]
                ext = jnp.concatenate([xtail_ref[s, :, ref_lanes], raw_h], axis=0)
                xtail_ref[s, :, ref_lanes] = raw_h[ts - SUBLANES:, :]
                y = pltpu.roll(ext, CONV_WIDTH - 1, 0)[SUBLANES:] * cw[0:1, lanes]
                for tap in range(1, CONV_WIDTH - 1):
                    y = y + pltpu.roll(ext, CONV_WIDTH - 1 - tap, 0)[SUBLANES:] * cw[tap:tap + 1, lanes]
                heads_out.append(_silu(y + raw_h * cw[CONV_WIDTH - 1:CONV_WIDTH, lanes]))
            conv.append(jnp.concatenate(heads_out, axis=1))
        qkv.append(jnp.concatenate(conv, axis=0))
        gate_sig.append(gate_group(part))
        if part == 1:
            beta_t, gr = write_strength_and_decay()
    q_all = qkv[0] * (lax.rsqrt(_head_sums(qkv[0] * qkv[0], ones2) + RMS_EPS) * (DN_HEAD_DIM ** -0.5))
    k_all = qkv[1] * lax.rsqrt(_head_sums(qkv[1] * qkv[1], ones2) + RMS_EPS)
    v_all = qkv[2]

    p = project(W_P0, POOL_WIDTH)
    pooled = []
    for s in seqs:
        p_s = p[rows_of(s)]
        tail = ptail_ref[s]
        ptail_ref[s] = p_s[ts - MAX_WINDOW:, :]
        pooled_s = []
        for gi, win in enumerate(POOL_WINDOWS):
            lo = gi * POOL_GROUP_DIM
            pg = p_s[:, lo:lo + POOL_GROUP_DIM]
            wsum = _dot(inwin_ref[gi], pg.astype(BF16))
            top = wsum[:MAX_WINDOW] + _dot(intail_ref[gi], tail[:, lo:lo + POOL_GROUP_DIM].astype(BF16))
            wsum = jnp.concatenate([top, wsum[MAX_WINDOW:]], axis=0)
            count = jnp.minimum(t_abs + 1, win).astype(F32)
            pooled_s.append(wsum / count - pg)
        pooled.append(jnp.concatenate(pooled_s, axis=1))
    pooled = jnp.concatenate(pooled, axis=0).astype(BF16)
    y_pool = _slab_dot(pooled, pproj_bf)
    z = project(W_Z0, DN_WIDTH)
    gate_sig += [gate_group(j) for j in range(QKV_PARTS, GATE_GROUPS)]

    chains = [(s, hd) for s in seqs for hd in range(DN_HEADS)]
    ids = range(len(chains))
    qn, kn_b, k_beta, v_beta, eg_b, kd_t, decay_p = [], [], [], [], [], [], []
    for s, hd in chains:
        lanes = slice(hd * DN_HEAD_DIM, (hd + 1) * DN_HEAD_DIM)
        qn.append(q_all[rows_of(s), lanes])
        kn = k_all[rows_of(s), lanes]
        v = v_all[rows_of(s), lanes]
        beta_b = column_broadcast(beta_t[hd:hd + 1, rows_of(s)])
        gr_row = gr[DN_HEADS + hd:DN_HEADS + hd + 1, rows_of(s)]
        gc_b = column_broadcast(gr_row)
        gc_p = jnp.concatenate(
            [jnp.where(low_lanes, gc_b[2 * i * CHUNK:(2 * i + 1) * CHUNK],
                       gc_b[(2 * i + 1) * CHUNK:(2 * i + 2) * CHUNK])
             for i in range(MIX_CHUNKS // 2)], axis=1)
        decay_p.append(jnp.exp(jnp.where(causal_p, gc_p - gr_row, -jnp.inf)))
        eg_b.append(jnp.exp(gc_b))
        gl_b = jnp.concatenate(
            [jnp.broadcast_to(gc_b[c * CHUNK + CHUNK - 1:(c + 1) * CHUNK, :], (CHUNK, DN_HEAD_DIM))
             for c in range(MIX_CHUNKS)], axis=0)
        k_beta.append(kn * beta_b)
        v_beta.append(v * beta_b)
        kn_b.append(kn.astype(BF16))
        kd_t.append((kn * jnp.exp(gl_b - gc_b)).T.astype(BF16))
    gram = [_dot_nt(jnp.concatenate([k_beta[i], qn[i]], axis=0).astype(BF16), kn_b[i]) for i in ids]
    lower = [pack(gram[i][:ts]) * jnp.where(strict_p, decay_p[i], 0.0) for i in ids]
    qk = [block_diag(pack(gram[i][ts:]) * decay_p[i]) for i in ids]
    t_mat = [eye_p - lower[i] for i in ids]
    power = [_dot(lower[i].astype(BF16), block_diag(lower[i])) for i in ids]
    for _ in range(NEUMANN_LEVELS - 2):
        both = [_dot(jnp.concatenate([power[i], t_mat[i]], axis=0).astype(BF16), block_diag(power[i]))
                for i in ids]
        power = [both[i][:CHUNK] for i in ids]
        t_mat = [t_mat[i] + both[i][CHUNK:] for i in ids]
    t_mat = [t_mat[i] + _dot(t_mat[i].astype(BF16), block_diag(power[i])) for i in ids]
    uw = [_dot(block_diag(t_mat[i]),
               jnp.concatenate([v_beta[i], k_beta[i] * eg_b[i]], axis=1).astype(BF16)) for i in ids]
    q_dec = [qn[i] * eg_b[i] for i in ids]
    state = [state_ref[s, hd] for s, hd in chains]
    kd_uw = []
    for i in ids:
        per_chunk = []
        for c in range(MIX_CHUNKS):
            uw_pad = jnp.concatenate(
                [uw[i][cc * CHUNK:(cc + 1) * CHUNK] if cc == c else jnp.zeros((CHUNK, 2 * DN_HEAD_DIM), F32)
                 for cc in range(MIX_CHUNKS)], axis=0).astype(BF16)
            per_chunk.append(_dot(kd_t[i], uw_pad))
        kd_uw.append(per_chunk)
    o_inter = [[] for _ in ids]
    v_new_all = [[] for _ in ids]
    for c in range(MIX_CHUNKS):
        r0 = c * CHUNK
        for i in ids:
            u_c = uw[i][r0:r0 + CHUNK, :DN_HEAD_DIM]
            w_c = uw[i][r0:r0 + CHUNK, DN_HEAD_DIM:]
            lhs = jnp.concatenate(
                [kd_uw[i][c][:, DN_HEAD_DIM:], w_c, q_dec[i][r0:r0 + CHUNK]], axis=0).astype(BF16)
            prod = _dot(lhs, state[i].astype(BF16))
            v_new_all[i].append(u_c - prod[DN_HEAD_DIM:DN_HEAD_DIM + CHUNK])
            o_inter[i].append(prod[DN_HEAD_DIM + CHUNK:])
            g_last = eg_b[i][r0 + CHUNK - 1:r0 + CHUNK, :]
            state[i] = state[i] * g_last - prod[:DN_HEAD_DIM] + kd_uw[i][c][:, :DN_HEAD_DIM]
    o_chain = []
    for i, (s, hd) in enumerate(chains):
        state_ref[s, hd] = state[i]
        v_new_full = jnp.concatenate(v_new_all[i], axis=0).astype(BF16)
        o_chain.append(jnp.concatenate(o_inter[i], axis=0) + _dot(qk[i], v_new_full))
    o_all = jnp.concatenate(
        [jnp.concatenate(o_chain[s * DN_HEADS:(s + 1) * DN_HEADS], axis=1) for s in seqs], axis=0)
    o_all = o_all * lax.rsqrt(_head_sums(o_all * o_all, ones2) * (1.0 / DN_HEAD_DIM) + RMS_EPS)
    o_all = o_all * dng_ref[...] * _silu(z)
    y_dn = _slab_dot(o_all.astype(BF16), dnproj_bf)

    half = GATE_GROUPS // 2
    merged = (jnp.concatenate(gate_sig[:half], axis=1) * y_dn
              + jnp.concatenate(gate_sig[half:], axis=1) * y_pool)
    m = _slab_dot(merged.astype(BF16), wout_bf)
    o_ref[...] = _layer_norm(ALPHA * h + m, lng_ref[...], lnb_ref[...]).reshape(nb, ts, D_MODEL)


def _mixer_ln(h3d, win_t, convw, prm, dng, dnproj, poolw, pscale, pproj, wout, lng, lnb):
    b, s, _ = h3d.shape
    assert win_t.shape == (W_GATE0 + 2 * D_MODEL, D_MODEL) and 2 * DN_HEADS == SUBLANES
    assert dnproj.shape == pproj.shape == (PREP_ROWS, D_MODEL) and wout.shape == (D_MODEL, D_MODEL)
    operands = (win_t, convw, prm, dng, dnproj, poolw, pscale, pproj, wout, lng, lnb, *_mixer_masks())
    in_hbm = (win_t, dnproj, pproj, wout)
    block = (MIX_BATCH, MIX_ROWS, D_MODEL)
    slabs = D_MODEL // SLAB_COLS
    return pl.pallas_call(
        _mixer_kernel,
        grid=(b // MIX_BATCH, s // MIX_ROWS),
        in_specs=[pl.BlockSpec(block, lambda i, j: (i, j, 0))]
        + [pl.BlockSpec(memory_space=pl.ANY) if any(op is w for w in in_hbm) else _resident(op.shape)
           for op in operands],
        out_specs=pl.BlockSpec(block, lambda i, j: (i, j, 0)),
        out_shape=jax.ShapeDtypeStruct(h3d.shape, F32),
        scratch_shapes=[
            pltpu.VMEM((MIX_BATCH, DN_HEADS, DN_HEAD_DIM, DN_HEAD_DIM), F32),
            pltpu.VMEM((MIX_BATCH, SUBLANES, QKV_COLS), F32),
            pltpu.VMEM((MIX_BATCH, MAX_WINDOW, POOL_WIDTH), F32),
            pltpu.VMEM((len(W_PIECES), D_MODEL, PREP_ROWS), BF16),
            pltpu.VMEM((SUBLANES, D_MODEL), F32),
            pltpu.VMEM((slabs, PREP_ROWS, SLAB_COLS), BF16),
            pltpu.VMEM((slabs, PREP_ROWS, SLAB_COLS), BF16),
            pltpu.VMEM((slabs, D_MODEL, SLAB_COLS), BF16),
            pltpu.VMEM((2, PREP_ROWS, D_MODEL), F32),
            pltpu.SemaphoreType.DMA((3,)),
        ],
        compiler_params=pltpu.CompilerParams(
            dimension_semantics=("arbitrary", "arbitrary"), vmem_limit_bytes=MIX_VMEM_LIMIT),
        name="mixer_ln",
    )(h3d, *operands)


def _row(v):
    return v.reshape(1, -1).astype(F32)


def kernel(x, ffn_pre_w_gate, ffn_pre_w_up, ffn_pre_w_down, norm_pre_g, norm_pre_b, mix_w_in, mix_conv_w, dn_a_log, dn_dt_bias, dn_norm_g, dn_w_proj, pool_w, pool_scale, pool_w_proj, mix_w_out, norm_mix_g, norm_mix_b, ffn_post_w_gate, ffn_post_w_up, ffn_post_w_down, norm_post_g, norm_post_b):
    bsz, seq, _ = x.shape
    h = x
    for l in range(DEPTH):
        h = _ffn_ln(h.reshape(bsz * seq, D_MODEL),
                    ffn_pre_w_gate[l], ffn_pre_w_up[l], ffn_pre_w_down[l], _row(norm_pre_g[l]), _row(norm_pre_b[l]),
                    "ffn_ln_pre")

        lanes = MIX_BATCH * MIX_ROWS
        prm = jnp.zeros((2, SUBLANES, lanes), F32)
        prm = prm.at[0, DN_HEADS:2 * DN_HEADS].set(jnp.broadcast_to(dn_a_log[l][:, None], (DN_HEADS, lanes)))
        prm = prm.at[1, DN_HEADS:2 * DN_HEADS].set(jnp.broadcast_to(dn_dt_bias[l][:, None], (DN_HEADS, lanes)))
        h = _mixer_ln(
            h.reshape(bsz, seq, D_MODEL),
            jnp.swapaxes(mix_w_in[l], 0, 1),
            mix_conv_w[l].astype(F32), prm, _row(jnp.tile(dn_norm_g[l], DN_HEADS)),
            dn_w_proj[l], pool_w[l], _row(pool_scale[l]), pool_w_proj[l], mix_w_out[l],
            _row(norm_mix_g[l]), _row(norm_mix_b[l]))

        h = _ffn_ln(h.reshape(bsz * seq, D_MODEL),
                    ffn_post_w_gate[l], ffn_post_w_up[l], ffn_post_w_down[l], _row(norm_post_g[l]), _row(norm_post_b[l]),
                    "ffn_ln_post")
    return h.reshape(bsz, seq, D_MODEL)
```
